```python
import math
import jax, jax.numpy as jnp
from jax import lax
import numpy as np

D_MODEL = 1024
BATCH = 32
SEQ = 2048
DEPTH = 4

N_META = 16
GRID_W = 64
NA_HEADS = 16
NA_HEAD_DIM = 64
NA_KH_MAX = 8
NA_KW = 16
NA_Q_COLS = 16
NA_K_COLS = 32
SWA_Q_HEADS = 16
SWA_KV_HEADS = 4
SWA_GROUP = SWA_Q_HEADS // SWA_KV_HEADS
SWA_HEAD_DIM = 64
SWA_WINDOW = 128
SWA_BLOCK = 128
T5_BUCKETS = 32
T5_MAX_DIST = 128
FFN_DIM = 3584
N_EXPERTS = 8
TOP_K = 2
MOE_BLOCK = 512
EPS = 1e-6
NEG_INF = -1e30

kernel_name = 'hybrid_na_swa_moe_encoder'


def rmsnorm(x, g):
    xf = x.astype(jnp.float32)
    y = xf * lax.rsqrt(jnp.mean(xf * xf, axis=-1, keepdims=True) + EPS)
    return (y * g.astype(jnp.float32)).astype(x.dtype)


def t5_bucket(rel):
    nb = T5_BUCKETS // 2
    max_exact = nb // 2
    ret = jnp.where(rel > 0, nb, 0)
    n = jnp.abs(rel)
    nf = jnp.maximum(n, max_exact).astype(jnp.float32)
    large = max_exact + (jnp.log(nf / max_exact) / math.log(T5_MAX_DIST / max_exact) * (nb - max_exact)).astype(jnp.int32)
    large = jnp.minimum(large, nb - 1)
    return ret + jnp.where(n < max_exact, n, large)


def sink_probs(logits, sink):
    s = sink[:, :, None, None]
    m = jnp.maximum(jnp.max(logits, axis=-1, keepdims=True), s)
    e = jnp.exp(logits - m)
    return e / (jnp.sum(e, axis=-1, keepdims=True) + jnp.exp(s - m))


def swiglu(h, w_gate, w_up, w_down):
    return (jax.nn.silu(h @ w_gate) * (h @ w_up)) @ w_down


def na_mixer(h, w_qkv, q_g, k_g, rpb, w_o):
    B, L, _ = h.shape
    S = L - N_META
    rows = S // GRID_W
    kh = min(NA_KH_MAX, rows)
    n_cb = GRID_W // NA_Q_COLS
    hd = NA_HEADS * NA_HEAD_DIM
    qkv = h @ w_qkv
    q = qkv[..., :hd].reshape(B, L, NA_HEADS, NA_HEAD_DIM)
    k = qkv[..., hd:2 * hd].reshape(B, L, NA_HEADS, NA_HEAD_DIM)
    v = qkv[..., 2 * hd:].reshape(B, L, NA_HEADS, NA_HEAD_DIM)
    q = rmsnorm(q, q_g) * (NA_HEAD_DIM ** -0.5)
    k = rmsnorm(k, k_g)
    qm, km, vm = q[:, :N_META], k[:, :N_META], v[:, :N_META]
    lm = jnp.einsum('bqhd,bkhd->bhqk', qm, km, preferred_element_type=jnp.float32)
    om = jnp.einsum('bhqk,bkhd->bqhd', jax.nn.softmax(lm, axis=-1).astype(v.dtype), vm)
    qg = q[:, N_META:].reshape(B, rows, GRID_W, NA_HEADS, NA_HEAD_DIM)
    kg = k[:, N_META:].reshape(B, rows, GRID_W, NA_HEADS, NA_HEAD_DIM)
    vg = v[:, N_META:].reshape(B, rows, GRID_W, NA_HEADS, NA_HEAD_DIM)
    q_col = np.arange(GRID_W).reshape(n_cb, NA_Q_COLS)
    k_start = np.clip(np.arange(n_cb) * NA_Q_COLS - NA_KW // 2, 0, GRID_W - NA_K_COLS)
    k_col = k_start[:, None] + np.arange(NA_K_COLS)
    w_start = np.clip(q_col - NA_KW // 2, 0, GRID_W - NA_KW)
    col_ok = (k_col[:, None, :] >= w_start[:, :, None]) & (k_col[:, None, :] < w_start[:, :, None] + NA_KW)
    nk = kh * NA_K_COLS
    mask = np.broadcast_to(col_ok[:, :, None, :], (n_cb, NA_Q_COLS, kh, NA_K_COLS)).reshape(n_cb, NA_Q_COLS, nk)
    dc = np.clip(k_col[:, None, :] - q_col[:, :, None] + NA_KW - 1, 0, 2 * NA_KW - 2)

    def row_block(r):
        rs = jnp.clip(r - kh // 2, 0, rows - kh)
        kb = lax.dynamic_slice_in_dim(kg, rs, kh, axis=1)[:, :, k_col]
        vb = lax.dynamic_slice_in_dim(vg, rs, kh, axis=1)[:, :, k_col]
        kb = kb.transpose(0, 2, 1, 3, 4, 5).reshape(B, n_cb, nk, NA_HEADS, NA_HEAD_DIM)
        vb = vb.transpose(0, 2, 1, 3, 4, 5).reshape(B, n_cb, nk, NA_HEADS, NA_HEAD_DIM)
        qr = lax.dynamic_index_in_dim(qg, r, axis=1, keepdims=False).reshape(B, n_cb, NA_Q_COLS, NA_HEADS, NA_HEAD_DIM)
        dr = rs - r + jnp.arange(kh) + NA_KH_MAX - 1
        bias = rpb[:, dr][:, :, dc].transpose(0, 2, 3, 1, 4).reshape(NA_HEADS, n_cb, NA_Q_COLS, nk)
        lw = jnp.einsum('bcqhd,bckhd->bhcqk', qr, kb, preferred_element_type=jnp.float32)
        lw = jnp.where(mask, lw + bias.astype(jnp.float32), NEG_INF)
        lmk = jnp.einsum('bcqhd,bmhd->bhcqm', qr, km, preferred_element_type=jnp.float32)
        p = jax.nn.softmax(jnp.concatenate([lmk, lw], axis=-1), axis=-1).astype(v.dtype)
        o = (jnp.einsum('bhcqm,bmhd->bcqhd', p[..., :N_META], vm)
             + jnp.einsum('bhcqk,bckhd->bcqhd', p[..., N_META:], vb))
        return o.reshape(B, GRID_W, NA_HEADS, NA_HEAD_DIM)

    og = lax.map(row_block, jnp.arange(rows))
    og = jnp.moveaxis(og, 0, 1).reshape(B, S, hd)
    o = jnp.concatenate([om.reshape(B, N_META, hd), og], axis=1)
    return o @ w_o


def swa_mixer(h, w_qkv, q_g, k_g, sink, t5_bias, w_o):
    B, L, _ = h.shape
    S = L - N_META
    nb = S // SWA_BLOCK
    nq = SWA_Q_HEADS * SWA_HEAD_DIM
    nkv = SWA_KV_HEADS * SWA_HEAD_DIM
    qkv = h @ w_qkv
    q = qkv[..., :nq].reshape(B, L, SWA_KV_HEADS, SWA_GROUP, SWA_HEAD_DIM)
    k = qkv[..., nq:nq + nkv].reshape(B, L, SWA_KV_HEADS, SWA_HEAD_DIM)
    v = qkv[..., nq + nkv:].reshape(B, L, SWA_KV_HEADS, SWA_HEAD_DIM)
    q = rmsnorm(q, q_g) * (SWA_HEAD_DIM ** -0.5)
    k = rmsnorm(k, k_g)
    s = sink.astype(jnp.float32).reshape(SWA_KV_HEADS, SWA_GROUP)

    def head_bias(rel):
        b = t5_bias[t5_bucket(rel)].astype(jnp.float32)
        return jnp.moveaxis(b, -1, 0).reshape(SWA_KV_HEADS, SWA_GROUP, rel.shape[0], rel.shape[1])

    qm, km, vm = q[:, :N_META], k[:, :N_META], v[:, :N_META]
    pos_m = jnp.arange(N_META)
    lm = jnp.einsum('bqngd,bknd->bngqk', qm, km, preferred_element_type=jnp.float32)
    lm = lm + head_bias(pos_m[None, :] - pos_m[:, None])
    om = jnp.einsum('bngqk,bknd->bqngd', sink_probs(lm, s).astype(v.dtype), vm)
    qr = q[:, N_META:]
    pad = ((0, 0), (SWA_BLOCK, SWA_BLOCK), (0, 0), (0, 0))
    kp = jnp.pad(k[:, N_META:], pad)
    vp = jnp.pad(v[:, N_META:], pad)
    j = jnp.arange(3 * SWA_BLOCK)
    qi = jnp.arange(SWA_BLOCK)
    rel_w = j[None, :] - SWA_BLOCK - qi[:, None]
    bias_w = head_bias(rel_w)
    band = jnp.abs(rel_w) <= SWA_WINDOW

    def block(i):
        qb = lax.dynamic_slice_in_dim(qr, i * SWA_BLOCK, SWA_BLOCK, axis=1)
        kb = lax.dynamic_slice_in_dim(kp, i * SWA_BLOCK, 3 * SWA_BLOCK, axis=1)
        vb = lax.dynamic_slice_in_dim(vp, i * SWA_BLOCK, 3 * SWA_BLOCK, axis=1)
        kidx = i * SWA_BLOCK - SWA_BLOCK + j
        ok = band & ((kidx >= 0) & (kidx < S))[None, :]
        lw = jnp.einsum('bqngd,bknd->bngqk', qb, kb, preferred_element_type=jnp.float32)
        lw = jnp.where(ok, lw + bias_w, NEG_INF)
        qpos = N_META + i * SWA_BLOCK + qi
        lmk = jnp.einsum('bqngd,bmnd->bngqm', qb, km, preferred_element_type=jnp.float32)
        lmk = lmk + head_bias(pos_m[None, :] - qpos[:, None])
        p = sink_probs(jnp.concatenate([lmk, lw], axis=-1), s).astype(v.dtype)
        o = (jnp.einsum('bngqm,bmnd->bqngd', p[..., :N_META], vm)
             + jnp.einsum('bngqk,bknd->bqngd', p[..., N_META:], vb))
        return o.reshape(B, SWA_BLOCK, nq)

    ob = lax.map(block, jnp.arange(nb))
    o_real = jnp.moveaxis(ob, 0, 1).reshape(B, S, nq)
    o = jnp.concatenate([om.reshape(B, N_META, nq), o_real], axis=1)
    return o @ w_o


def moe_ffn(h, w_router, w_gate, w_up, w_down):
    B, L, D = h.shape
    xf = h.reshape(-1, D)
    N = xf.shape[0]
    logits = jnp.matmul(xf, w_router, preferred_element_type=jnp.float32)
    top_val, top_idx = lax.top_k(logits, TOP_K)
    gates = jax.nn.softmax(top_val, axis=-1)
    A = N * TOP_K
    e_flat = top_idx.reshape(-1).astype(jnp.int32)
    tok_flat = jnp.repeat(jnp.arange(N, dtype=jnp.int32), TOP_K)
    g_flat = gates.reshape(-1)
    order = jnp.argsort(e_flat)
    e_s, tok_s, g_s = e_flat[order], tok_flat[order], g_flat[order]
    counts = jnp.zeros((N_EXPERTS,), jnp.int32).at[e_flat].add(1)
    padded = (counts + MOE_BLOCK - 1) // MOE_BLOCK * MOE_BLOCK
    start = jnp.cumsum(counts) - counts
    pend = jnp.cumsum(padded)
    pstart = pend - padded
    dest = pstart[e_s] + (jnp.arange(A, dtype=jnp.int32) - start[e_s])
    n_blk = -(-(A + N_EXPERTS * (MOE_BLOCK - 1)) // MOE_BLOCK)
    P = n_blk * MOE_BLOCK
    row_tok = jnp.zeros((P,), jnp.int32).at[dest].set(tok_s)
    row_gate = jnp.zeros((P,), jnp.float32).at[dest].set(g_s)
    blk_exp = jnp.minimum(jnp.searchsorted(pend, jnp.arange(n_blk, dtype=jnp.int32) * MOE_BLOCK, side='right'), N_EXPERTS - 1)

    def expert_block(args):
        toks, e, g = args
        xb = xf[toks]
        hb = jax.nn.silu(xb @ w_gate[e]) * (xb @ w_up[e])
        return (hb @ w_down[e]) * g[:, None].astype(xf.dtype)

    yb = lax.map(expert_block, (row_tok.reshape(n_blk, MOE_BLOCK), blk_exp, row_gate.reshape(n_blk, MOE_BLOCK)))
    y = jnp.zeros_like(xf).at[row_tok].add(yb.reshape(P, D).astype(xf.dtype))
    return y.reshape(B, L, D)


def setup_inputs(seed: int = 0) -> dict:
    key = jax.random.key(seed)
    ks = iter(jax.random.split(key, 64))
    D = D_MODEL

    def nrm(shape, scale):
        return jax.random.normal(next(ks), shape, jnp.float32) * scale

    def gain(n):
        return 1.0 + 0.05 * jax.random.normal(next(ks), (n,), jnp.float32)

    inp = {}
    inp['x'] = nrm((BATCH, SEQ, D), 1.0)
    inp['meta_tokens'] = nrm((N_META, D), 1.0)
    inp['t5_bias'] = nrm((T5_BUCKETS, SWA_Q_HEADS), 0.1)
    for i in range(DEPTH):
        p = 'l%d_' % i
        inp[p + 'norm_mix'] = gain(D)
        if i % 2 == 0:
            hd = NA_HEADS * NA_HEAD_DIM
            inp[p + 'w_qkv'] = nrm((D, 3 * hd), D ** -0.5)
            inp[p + 'q_norm'] = gain(NA_HEAD_DIM)
            inp[p + 'k_norm'] = gain(NA_HEAD_DIM)
            inp[p + 'rpb'] = nrm((NA_HEADS, 2 * NA_KH_MAX - 1, 2 * NA_KW - 1), 0.1)
            inp[p + 'w_o'] = nrm((hd, D), hd ** -0.5)
        else:
            nq = SWA_Q_HEADS * SWA_HEAD_DIM
            nqkv = (SWA_Q_HEADS + 2 * SWA_KV_HEADS) * SWA_HEAD_DIM
            inp[p + 'w_qkv'] = nrm((D, nqkv), D ** -0.5)
            inp[p + 'q_norm'] = gain(SWA_HEAD_DIM)
            inp[p + 'k_norm'] = gain(SWA_HEAD_DIM)
            inp[p + 'sink'] = nrm((SWA_Q_HEADS,), 0.5)
            inp[p + 'w_o'] = nrm((nq, D), nq ** -0.5)
        inp[p + 'norm_ffn'] = gain(D)
        if i % 2 == 0:
            inp[p + 'w_gate'] = nrm((D, FFN_DIM), D ** -0.5)
            inp[p + 'w_up'] = nrm((D, FFN_DIM), D ** -0.5)
            inp[p + 'w_down'] = nrm((FFN_DIM, D), FFN_DIM ** -0.5)
        else:
            inp[p + 'w_router'] = nrm((D, N_EXPERTS), D ** -0.5)
            inp[p + 'w_gate'] = nrm((N_EXPERTS, D, FFN_DIM), D ** -0.5)
            inp[p + 'w_up'] = nrm((N_EXPERTS, D, FFN_DIM), D ** -0.5)
            inp[p + 'w_down'] = nrm((N_EXPERTS, FFN_DIM, D), FFN_DIM ** -0.5)
    return inp


def reference(x, meta_tokens, t5_bias,
              l0_norm_mix, l0_w_qkv, l0_q_norm, l0_k_norm, l0_rpb, l0_w_o, l0_norm_ffn, l0_w_gate, l0_w_up, l0_w_down,
              l1_norm_mix, l1_w_qkv, l1_q_norm, l1_k_norm, l1_sink, l1_w_o, l1_norm_ffn, l1_w_router, l1_w_gate, l1_w_up, l1_w_down,
              l2_norm_mix, l2_w_qkv, l2_q_norm, l2_k_norm, l2_rpb, l2_w_o, l2_norm_ffn, l2_w_gate, l2_w_up, l2_w_down,
              l3_norm_mix, l3_w_qkv, l3_q_norm, l3_k_norm, l3_sink, l3_w_o, l3_norm_ffn, l3_w_router, l3_w_gate, l3_w_up, l3_w_down):
    layers = [
        dict(norm_mix=l0_norm_mix, w_qkv=l0_w_qkv, q_norm=l0_q_norm, k_norm=l0_k_norm, rpb=l0_rpb, w_o=l0_w_o,
             norm_ffn=l0_norm_ffn, w_gate=l0_w_gate, w_up=l0_w_up, w_down=l0_w_down),
        dict(norm_mix=l1_norm_mix, w_qkv=l1_w_qkv, q_norm=l1_q_norm, k_norm=l1_k_norm, sink=l1_sink, w_o=l1_w_o,
             norm_ffn=l1_norm_ffn, w_router=l1_w_router, w_gate=l1_w_gate, w_up=l1_w_up, w_down=l1_w_down),
        dict(norm_mix=l2_norm_mix, w_qkv=l2_w_qkv, q_norm=l2_q_norm, k_norm=l2_k_norm, rpb=l2_rpb, w_o=l2_w_o,
             norm_ffn=l2_norm_ffn, w_gate=l2_w_gate, w_up=l2_w_up, w_down=l2_w_down),
        dict(norm_mix=l3_norm_mix, w_qkv=l3_w_qkv, q_norm=l3_q_norm, k_norm=l3_k_norm, sink=l3_sink, w_o=l3_w_o,
             norm_ffn=l3_norm_ffn, w_router=l3_w_router, w_gate=l3_w_gate, w_up=l3_w_up, w_down=l3_w_down),
    ]
    B = x.shape[0]
    meta = jnp.broadcast_to(meta_tokens[None].astype(x.dtype), (B, N_META, x.shape[-1]))
    h = jnp.concatenate([meta, x], axis=1)
    for i in range(DEPTH):
        p = layers[i]
        y = rmsnorm(h, p['norm_mix'])
        if i % 2 == 0:
            h = h + na_mixer(y, p['w_qkv'], p['q_norm'], p['k_norm'], p['rpb'], p['w_o'])
        else:
            h = h + swa_mixer(y, p['w_qkv'], p['q_norm'], p['k_norm'], p['sink'], t5_bias, p['w_o'])
        y = rmsnorm(h, p['norm_ffn'])
        if i % 2 == 0:
            h = h + swiglu(y, p['w_gate'], p['w_up'], p['w_down'])
        else:
            h = h + moe_ffn(y, p['w_router'], p['w_gate'], p['w_up'], p['w_down'])
    return h[:, N_META:]
```

```python
import math

import jax
import jax.numpy as jnp
import numpy as np
from jax import lax
from jax.experimental import pallas as pl
from jax.experimental.pallas import tpu as pltpu

D_MODEL = 1024
SEQ = 2048
N_META = 16
GRID_W = 64
NA_HEADS = 16
NA_KH = 8
NA_KW = 16
HEAD_DIM = 64
SWA_Q_HEADS = 16
SWA_KV_HEADS = 4
SWA_GROUP = SWA_Q_HEADS // SWA_KV_HEADS
SWA_WINDOW = 128
SWA_BLOCK = 128
T5_BUCKETS = 32
T5_MAX_DIST = 128
FFN_DIM = 3584
N_EXPERTS = 8
TOP_K = 2
EPS = 1e-6
NEG_INF = -1e30

LANES = 128
ROW_TILE = 1536
FFN_CHUNK = 512
MOE_TILE = 512
VMEM_LIMIT = 56 * 1024 * 1024

F32 = jnp.float32
BF16 = jnp.bfloat16
I32 = jnp.int32
U32 = jnp.uint32


def _params(sem, vmem=VMEM_LIMIT):
    return pltpu.CompilerParams(dimension_semantics=sem, vmem_limit_bytes=vmem)


def _rms_rows(x, g):
    return (x * lax.rsqrt(jnp.mean(x * x, axis=-1, keepdims=True) + EPS)) * g


def _norm_matmul_kernel(h_ref, g_ref, w_ref, o_ref, y_scr):
    @pl.when(pl.program_id(1) == 0)
    def _():
        y_scr[...] = _rms_rows(h_ref[...], g_ref[...]).astype(BF16)

    o_ref[...] = jnp.dot(y_scr[...], w_ref[...], preferred_element_type=F32).astype(o_ref.dtype)


def norm_matmul(h, g, w, tn):
    nt, d = h.shape
    nout = w.shape[1]
    return pl.pallas_call(
        _norm_matmul_kernel,
        out_shape=jax.ShapeDtypeStruct((nt, nout), BF16),
        grid=(nt // ROW_TILE, nout // tn),
        in_specs=[
            pl.BlockSpec((ROW_TILE, d), lambda i, j: (i, 0)),
            pl.BlockSpec((1, d), lambda i, j: (0, 0)),
            pl.BlockSpec((d, tn), lambda i, j: (0, j)),
        ],
        out_specs=pl.BlockSpec((ROW_TILE, tn), lambda i, j: (i, j)),
        scratch_shapes=[pltpu.VMEM((ROW_TILE, d), BF16)],
        compiler_params=_params(("parallel", "arbitrary")),
        name="norm_matmul",
    )(h, g.reshape(1, d), w)


def _pair_norm(x, g, lo):
    sq = x * x
    s_lo = jnp.sum(jnp.where(lo, sq, 0.0), axis=-1, keepdims=True)
    s_hi = jnp.sum(jnp.where(lo, 0.0, sq), axis=-1, keepdims=True)
    ms = jnp.where(lo, s_lo, s_hi) * (1.0 / HEAD_DIM)
    return (x * lax.rsqrt(ms + EPS)) * g


def _dot_t(a, b):
    return lax.dot_general(a, b, (((1,), (1,)), ((), ())), preferred_element_type=F32)


NA_ROWS = SEQ // GRID_W
NA_NK = NA_KH * GRID_W
NA_VARIANTS = 8


def _na_kernel(q_ref, k_ref, v_ref, qm_ref, km_ref, vm_ref, qg_ref, kg_ref, bias_ref,
               o_ref, om_ref, qn_scr, kn_scr):
    lo = lax.broadcasted_iota(I32, (1, LANES), 1) < HEAD_DIM
    scale = HEAD_DIM ** -0.5
    qn_scr[...] = (_pair_norm(q_ref[...].astype(F32), qg_ref[...], lo) * scale).astype(BF16)
    kn_scr[...] = _pair_norm(k_ref[...].astype(F32), kg_ref[...], lo).astype(BF16)
    kmn = _pair_norm(km_ref[...].astype(F32), kg_ref[...], lo).astype(BF16)
    qmn = (_pair_norm(qm_ref[...].astype(F32), qg_ref[...], lo) * scale).astype(BF16)
    vm = vm_ref[...]
    zero = jnp.zeros((), BF16)

    def stack(q):
        return jnp.concatenate([jnp.where(lo, q, zero), jnp.where(lo, zero, q)], axis=0)

    lm = _dot_t(stack(qmn), kmn)
    em = jnp.exp(lm - jnp.max(lm, axis=-1, keepdims=True))
    pm = em / jnp.sum(em, axis=-1, keepdims=True)
    omm = jnp.dot(pm.astype(BF16), vm, preferred_element_type=F32)
    om_ref[...] = jnp.where(lo, omm[:N_META], omm[N_META:]).astype(om_ref.dtype)

    def row(r, carry):
        rs = jnp.clip(r - NA_KH // 2, 0, NA_ROWS - NA_KH)
        var = jnp.where(r < 4, r, jnp.where(r > NA_ROWS - 4, r - (NA_ROWS - NA_VARIANTS), 4))
        q0 = pl.multiple_of(r * GRID_W, GRID_W)
        k0 = pl.multiple_of(rs * GRID_W, GRID_W)
        lhs = stack(qn_scr[pl.ds(q0, GRID_W), :])
        kw = kn_scr[pl.ds(k0, NA_NK), :]
        vw = v_ref[pl.ds(k0, NA_NK), :]
        s = _dot_t(lhs, kw) + bias_ref[0, var]
        sm = _dot_t(lhs, kmn)
        m = jnp.maximum(jnp.max(s, axis=-1, keepdims=True), jnp.max(sm, axis=-1, keepdims=True))
        e = jnp.exp(s - m)
        em_ = jnp.exp(sm - m)
        den = jnp.sum(e, axis=-1, keepdims=True) + jnp.sum(em_, axis=-1, keepdims=True)
        o = (jnp.dot(em_.astype(BF16), vm, preferred_element_type=F32)
             + jnp.dot(e.astype(BF16), vw, preferred_element_type=F32)) / den
        o_ref[pl.ds(q0, GRID_W), :] = jnp.where(lo, o[:GRID_W], o[GRID_W:]).astype(o_ref.dtype)
        return carry

    lax.fori_loop(0, NA_ROWS, row, 0)


def _na_bias_tables(rpb):
    qc = np.arange(GRID_W)[:, None]
    kc = np.arange(GRID_W)[None, :]
    w_start = np.clip(qc - NA_KW // 2, 0, GRID_W - NA_KW)
    ok = (kc >= w_start) & (kc < w_start + NA_KW)
    dc = np.clip(kc - qc + NA_KW - 1, 0, 2 * NA_KW - 2)
    onehot = (dc[None] == np.arange(2 * NA_KW - 1)[:, None, None]) & ok[None]
    onehot = jnp.asarray(onehot.reshape(2 * NA_KW - 1, -1), F32)
    toe = jnp.dot(rpb.reshape(-1, 2 * NA_KW - 1).astype(F32), onehot, precision=lax.Precision.HIGHEST)
    toe = toe.reshape(NA_HEADS, 2 * NA_KH - 1, GRID_W, GRID_W)
    toe = jnp.where(jnp.asarray(ok)[None, None], toe, NEG_INF)
    reps = [0, 1, 2, 3, NA_ROWS // 2, NA_ROWS - 3, NA_ROWS - 2, NA_ROWS - 1]
    tabs = []
    for r in reps:
        rs = int(np.clip(r - NA_KH // 2, 0, NA_ROWS - NA_KH))
        tabs.append(jnp.concatenate([toe[:, rs - r + i + NA_KH - 1] for i in range(NA_KH)], axis=-1))
    t = jnp.stack(tabs, axis=1)
    t = t.reshape(NA_HEADS // 2, 2, NA_VARIANTS, GRID_W, NA_NK).transpose(0, 2, 1, 3, 4)
    return t.reshape(NA_HEADS // 2, NA_VARIANTS, 2 * GRID_W, NA_NK)


def na_attention(qkv, q_g, k_g, rpb, batch):
    npairs = NA_HEADS // 2
    mrow = batch * SEQ // N_META
    bias = _na_bias_tables(rpb)
    qg = jnp.tile(q_g.astype(F32), 2).reshape(1, LANES)
    kg = jnp.tile(k_g.astype(F32), 2).reshape(1, LANES)
    real = lambda off: pl.BlockSpec((SEQ, LANES), lambda j, b: (b, off + j))
    meta = lambda off: pl.BlockSpec((N_META, LANES), lambda j, b: (mrow + b, off + j))
    vec = pl.BlockSpec((1, LANES), lambda j, b: (0, 0))
    return pl.pallas_call(
        _na_kernel,
        out_shape=(jax.ShapeDtypeStruct((batch * SEQ, D_MODEL), BF16),
                   jax.ShapeDtypeStruct((batch * N_META, D_MODEL), BF16)),
        grid=(npairs, batch),
        in_specs=[real(0), real(npairs), real(2 * npairs), meta(0), meta(npairs), meta(2 * npairs), vec, vec,
                  pl.BlockSpec((1, NA_VARIANTS, 2 * GRID_W, NA_NK), lambda j, b: (j, 0, 0, 0))],
        out_specs=(pl.BlockSpec((SEQ, LANES), lambda j, b: (b, j)),
                   pl.BlockSpec((N_META, LANES), lambda j, b: (b, j))),
        scratch_shapes=[pltpu.VMEM((SEQ, LANES), BF16), pltpu.VMEM((SEQ, LANES), BF16)],
        compiler_params=_params(("parallel", "parallel")),
        name="na_attention",
    )(qkv, qkv, qkv, qkv, qkv, qkv, qg, kg, bias)


SWA_NB = SEQ // SWA_BLOCK
SWA_NK = 3 * SWA_BLOCK
SWA_ROWS = SWA_GROUP * SWA_BLOCK
SWA_MROWS = SWA_GROUP * N_META


def _swa_kernel(q_ref, k_ref, v_ref, qm_ref, km_ref, vm_ref, qg_ref, kg_ref, bw_ref, bm_ref, bmm_ref,
                sink_ref, sinkm_ref, o_ref, om_ref, qn_scr, kd_scr, vd_scr):
    n = pl.program_id(0)
    lane = lax.broadcasted_iota(I32, (1, LANES), 1)
    lo = lane < HEAD_DIM
    sel = (lane >= HEAD_DIM) == (n % 2 == 1)
    scale = HEAD_DIM ** -0.5
    zero = jnp.zeros((), BF16)

    def dup(x):
        xs = jnp.where(sel, x, 0.0)
        return xs + pltpu.roll(xs, HEAD_DIM, axis=1)

    kd_scr[...] = dup(_pair_norm(k_ref[...].astype(F32), kg_ref[...], lo)).astype(BF16)
    vd_scr[...] = dup(v_ref[...].astype(F32)).astype(BF16)
    kmd = dup(_pair_norm(km_ref[...].astype(F32), kg_ref[...], lo)).astype(BF16)
    vmd = dup(vm_ref[...].astype(F32)).astype(BF16)
    for c in range(2):
        cs = slice(c * LANES, (c + 1) * LANES)
        qn_scr[:, cs] = (_pair_norm(q_ref[:, cs].astype(F32), qg_ref[...], lo) * scale).astype(BF16)

    def stack(q):
        a, b = q[:, :LANES], q[:, LANES:]
        return jnp.concatenate([jnp.where(lo, a, zero), jnp.where(lo, zero, a),
                                jnp.where(lo, b, zero), jnp.where(lo, zero, b)], axis=0)

    def unstack(o, t):
        return jnp.concatenate([jnp.where(lo, o[:t], o[t:2 * t]), jnp.where(lo, o[2 * t:3 * t], o[3 * t:])], axis=1)

    qmn = jnp.concatenate(
        [(_pair_norm(qm_ref[:, c * LANES:(c + 1) * LANES].astype(F32), qg_ref[...], lo) * scale).astype(BF16)
         for c in range(2)], axis=1)
    sink_m = sinkm_ref[0]
    lm = _dot_t(stack(qmn), kmd) + bmm_ref[0]
    mm = jnp.maximum(jnp.max(lm, axis=-1, keepdims=True), sink_m)
    em = jnp.exp(lm - mm)
    pm = em / (jnp.sum(em, axis=-1, keepdims=True) + jnp.exp(sink_m - mm))
    om_ref[...] = unstack(jnp.dot(pm.astype(BF16), vmd, preferred_element_type=F32), N_META).astype(om_ref.dtype)

    sink = sink_ref[0]

    def block(i, carry):
        ws = jnp.clip(i - 1, 0, SWA_NB - 3)
        var = jnp.where(i == 0, 0, jnp.where(i == SWA_NB - 1, 2, 1))
        q0 = pl.multiple_of(i * SWA_BLOCK, SWA_BLOCK)
        k0 = pl.multiple_of(ws * SWA_BLOCK, SWA_BLOCK)
        lhs = stack(qn_scr[pl.ds(q0, SWA_BLOCK), :])
        kw = kd_scr[pl.ds(k0, SWA_NK), :]
        vw = vd_scr[pl.ds(k0, SWA_NK), :]
        s = _dot_t(lhs, kw) + bw_ref[0, var]
        sm = _dot_t(lhs, kmd) + bm_ref[0, jnp.minimum(i, 1)]
        m = jnp.maximum(jnp.maximum(jnp.max(s, axis=-1, keepdims=True), jnp.max(sm, axis=-1, keepdims=True)), sink)
        e = jnp.exp(s - m)
        em_ = jnp.exp(sm - m)
        den = jnp.sum(e, axis=-1, keepdims=True) + jnp.sum(em_, axis=-1, keepdims=True) + jnp.exp(sink - m)
        o = (jnp.dot(em_.astype(BF16), vmd, preferred_element_type=F32)
             + jnp.dot(e.astype(BF16), vw, preferred_element_type=F32)) / den
        o_ref[pl.ds(q0, SWA_BLOCK), :] = unstack(o, SWA_BLOCK).astype(o_ref.dtype)
        return carry

    lax.fori_loop(0, SWA_NB, block, 0)


def _t5_bucket(rel):
    nb = T5_BUCKETS // 2
    max_exact = nb // 2
    ret = jnp.where(rel > 0, nb, 0)
    n = jnp.abs(rel)
    nf = jnp.maximum(n, max_exact).astype(jnp.float32)
    large = max_exact + (jnp.log(nf / max_exact) / math.log(T5_MAX_DIST / max_exact) * (nb - max_exact)).astype(I32)
    large = jnp.minimum(large, nb - 1)
    return ret + jnp.where(n < max_exact, n, large)


def _swa_bias_tables(t5_bias):
    tb = t5_bias.astype(F32)

    def head_bias(rel):
        b = jnp.moveaxis(tb[_t5_bucket(rel)], -1, 0)
        return b.reshape(SWA_KV_HEADS, SWA_GROUP * rel.shape[0], rel.shape[1])

    qi = jnp.arange(SWA_BLOCK)[:, None]
    j = jnp.arange(SWA_NK)[None, :]
    wins = []
    for off in (0, SWA_BLOCK, 2 * SWA_BLOCK):
        rel = j - off - qi
        band = jnp.tile(jnp.abs(rel) <= SWA_WINDOW, (SWA_GROUP, 1))[None]
        wins.append(jnp.where(band, head_bias(rel), NEG_INF))
    bw = jnp.stack(wins, axis=1)
    pos_m = jnp.arange(N_META)[None, :]
    bm = jnp.stack([head_bias(pos_m - (N_META + blk * SWA_BLOCK + qi)) for blk in (0, 1)], axis=1)
    bmm = head_bias(pos_m - jnp.arange(N_META)[:, None])
    return bw, bm, bmm


def swa_attention(qkv, q_g, k_g, sink, t5_bias, batch):
    nq = SWA_Q_HEADS * HEAD_DIM
    mrow = batch * SEQ // N_META
    bw, bm, bmm = _swa_bias_tables(t5_bias)
    qg = jnp.tile(q_g.astype(F32), 2).reshape(1, LANES)
    kg = jnp.tile(k_g.astype(F32), 2).reshape(1, LANES)
    sink_hg = sink.astype(F32).reshape(SWA_KV_HEADS, SWA_GROUP)
    sink_rows = jnp.repeat(sink_hg, SWA_BLOCK, axis=1)[..., None]
    sink_meta = jnp.repeat(sink_hg, N_META, axis=1)[..., None]
    kcol = nq // LANES
    vcol = kcol + SWA_KV_HEADS * HEAD_DIM // LANES
    vec = pl.BlockSpec((1, LANES), lambda n, b: (0, 0))
    return pl.pallas_call(
        _swa_kernel,
        out_shape=(jax.ShapeDtypeStruct((batch * SEQ, D_MODEL), BF16),
                   jax.ShapeDtypeStruct((batch * N_META, D_MODEL), BF16)),
        grid=(SWA_KV_HEADS, batch),
        in_specs=[
            pl.BlockSpec((SEQ, 2 * LANES), lambda n, b: (b, n)),
            pl.BlockSpec((SEQ, LANES), lambda n, b: (b, kcol + n // 2)),
            pl.BlockSpec((SEQ, LANES), lambda n, b: (b, vcol + n // 2)),
            pl.BlockSpec((N_META, 2 * LANES), lambda n, b: (mrow + b, n)),
            pl.BlockSpec((N_META, LANES), lambda n, b: (mrow + b, kcol + n // 2)),
            pl.BlockSpec((N_META, LANES), lambda n, b: (mrow + b, vcol + n // 2)),
            vec, vec,
            pl.BlockSpec((1, 3, SWA_ROWS, SWA_NK), lambda n, b: (n, 0, 0, 0)),
            pl.BlockSpec((1, 2, SWA_ROWS, N_META), lambda n, b: (n, 0, 0, 0)),
            pl.BlockSpec((1, SWA_MROWS, N_META), lambda n, b: (n, 0, 0)),
            pl.BlockSpec((1, SWA_ROWS, 1), lambda n, b: (n, 0, 0)),
            pl.BlockSpec((1, SWA_MROWS, 1), lambda n, b: (n, 0, 0)),
        ],
        out_specs=(pl.BlockSpec((SEQ, 2 * LANES), lambda n, b: (b, n)),
                   pl.BlockSpec((N_META, 2 * LANES), lambda n, b: (b, n))),
        scratch_shapes=[pltpu.VMEM((SEQ, 2 * LANES), BF16), pltpu.VMEM((SEQ, LANES), BF16),
                        pltpu.VMEM((SEQ, LANES), BF16)],
        compiler_params=_params(("parallel", "parallel")),
        name="swa_attention",
    )(qkv, qkv, qkv, qkv, qkv, qkv, qg, kg, bw, bm, bmm, sink_rows, sink_meta)


def _attn_out_kernel(o_ref, w_ref, h_ref, out_ref):
    out_ref[...] = h_ref[...] + jnp.dot(o_ref[...], w_ref[...], preferred_element_type=F32)


def attn_out(o, w_o, h):
    nt, d = h.shape
    row = pl.BlockSpec((ROW_TILE, d), lambda i: (i, 0))
    return pl.pallas_call(
        _attn_out_kernel,
        out_shape=jax.ShapeDtypeStruct((nt, d), F32),
        grid=(nt // ROW_TILE,),
        in_specs=[row, pl.BlockSpec((d, d), lambda i: (0, 0)), row],
        out_specs=row,
        compiler_params=_params(("parallel",)),
        name="attn_out",
    )(o, w_o, h)


def _attn_out_router_kernel(o_ref, w_ref, h_ref, g_ref, wr_ref, out_ref, route_ref):
    hn = h_ref[...] + jnp.dot(o_ref[...], w_ref[...], preferred_element_type=F32)
    out_ref[...] = hn
    y = _rms_rows(hn, g_ref[...]).astype(BF16)
    lane = lax.broadcasted_iota(I32, (1, LANES), 1).astype(F32)
    logits = jnp.dot(y, wr_ref[...], preferred_element_type=F32)
    logits = jnp.where(lane < N_EXPERTS, logits, -jnp.inf)
    v1 = jnp.max(logits, axis=-1, keepdims=True)
    i1 = jnp.min(jnp.where(logits == v1, lane, float(LANES)), axis=-1, keepdims=True)
    rest = jnp.where(lane == i1, -jnp.inf, logits)
    v2 = jnp.max(rest, axis=-1, keepdims=True)
    i2 = jnp.min(jnp.where(rest == v2, lane, float(LANES)), axis=-1, keepdims=True)
    e2 = jnp.exp(v2 - v1)
    den = 1.0 + e2
    route = jnp.where(lane == 0, i1,
                      jnp.where(lane == 1, i2,
                                jnp.where(lane == 2, 1.0 / den, jnp.where(lane == 3, e2 / den, 0.0))))
    route_ref[...] = route


def attn_out_router(o, w_o, h, g, w_router):
    nt, d = h.shape
    wr = jnp.zeros((d, LANES), BF16).at[:, :N_EXPERTS].set(w_router.astype(BF16))
    row = pl.BlockSpec((ROW_TILE, d), lambda i: (i, 0))
    return pl.pallas_call(
        _attn_out_router_kernel,
        out_shape=(jax.ShapeDtypeStruct((nt, d), F32), jax.ShapeDtypeStruct((nt, LANES), F32)),
        grid=(nt // ROW_TILE,),
        in_specs=[row, pl.BlockSpec((d, d), lambda i: (0, 0)), row, pl.BlockSpec((1, d), lambda i: (0, 0)),
                  pl.BlockSpec((d, LANES), lambda i: (0, 0))],
        out_specs=(row, pl.BlockSpec((ROW_TILE, LANES), lambda i: (i, 0))),
        compiler_params=_params(("parallel",)),
        name="attn_out_router",
    )(o, w_o, h, g.reshape(1, d), wr)


def _silu_mul(g, u):
    return (g / (1.0 + jnp.exp(-g))) * u


def _dense_ffn_kernel(h_ref, g_ref, wg_ref, wu_ref, wd_ref, out_ref, y_scr, acc_scr):
    f = pl.program_id(1)

    @pl.when(f == 0)
    def _():
        y_scr[...] = _rms_rows(h_ref[...], g_ref[...]).astype(BF16)
        acc_scr[...] = jnp.zeros_like(acc_scr)

    y = y_scr[...]
    a = _silu_mul(jnp.dot(y, wg_ref[...], preferred_element_type=F32),
                  jnp.dot(y, wu_ref[...], preferred_element_type=F32)).astype(BF16)
    acc_scr[...] += jnp.dot(a, wd_ref[...], preferred_element_type=F32)

    @pl.when(f == pl.num_programs(1) - 1)
    def _():
        out_ref[...] = h_ref[...] + acc_scr[...]


def dense_ffn(h, g, w_gate, w_up, w_down):
    nt, d = h.shape
    row = pl.BlockSpec((ROW_TILE, d), lambda i, f: (i, 0))
    return pl.pallas_call(
        _dense_ffn_kernel,
        out_shape=jax.ShapeDtypeStruct((nt, d), F32),
        grid=(nt // ROW_TILE, FFN_DIM // FFN_CHUNK),
        in_specs=[row, pl.BlockSpec((1, d), lambda i, f: (0, 0)),
                  pl.BlockSpec((d, FFN_CHUNK), lambda i, f: (0, f)),
                  pl.BlockSpec((d, FFN_CHUNK), lambda i, f: (0, f)),
                  pl.BlockSpec((FFN_CHUNK, d), lambda i, f: (f, 0))],
        out_specs=row,
        scratch_shapes=[pltpu.VMEM((ROW_TILE, d), BF16), pltpu.VMEM((ROW_TILE, d), F32)],
        compiler_params=_params(("parallel", "arbitrary")),
        name="dense_ffn",
    )(h, g.reshape(1, d), w_gate, w_up, w_down)


HALF_D = D_MODEL // 2


def _pack_bf16_pairs(y):
    r = y.astype(BF16).astype(F32)
    hi = lax.bitcast_convert_type(r[:, :HALF_D], U32)
    low = lax.bitcast_convert_type(r[:, HALF_D:], U32)
    return hi | (low >> 16)


def _unpack_bf16_pairs(w):
    hi = lax.bitcast_convert_type(w & jnp.uint32(0xFFFF0000), F32)
    low = lax.bitcast_convert_type(w << 16, F32)
    return jnp.concatenate([hi, low], axis=1).astype(BF16)


def _dispatch_kernel(dest_hbm, h_ref, g_ref, zeros_hbm, xs_hbm, idx_smem, y_scr, sem_idx, sem_rows):
    del zeros_hbm
    i = pl.program_id(0)
    cp = pltpu.make_async_copy(dest_hbm.at[i], idx_smem, sem_idx)
    cp.start()
    y_scr[...] = _pack_bf16_pairs(_rms_rows(h_ref[...], g_ref[...]))
    cp.wait()

    def row_copy(t, k):
        d = idx_smem[TOP_K * t + k]
        return pltpu.make_async_copy(y_scr.at[pl.ds(t, 1), :], xs_hbm.at[pl.ds(d, 1), :], sem_rows)

    def issue(t, c):
        for k in range(TOP_K):
            row_copy(t, k).start()
        return c

    lax.fori_loop(0, MOE_TILE, issue, 0)

    def drain(t, c):
        for k in range(TOP_K):
            row_copy(t, k).wait()
        return c

    lax.fori_loop(0, MOE_TILE, drain, 0)


def moe_dispatch(h, g, dest, n_rows):
    nt, d = h.shape
    ntiles = nt // MOE_TILE
    zeros = jnp.zeros((n_rows, HALF_D), U32)
    return pl.pallas_call(
        _dispatch_kernel,
        out_shape=jax.ShapeDtypeStruct((n_rows, HALF_D), U32),
        grid=(ntiles,),
        in_specs=[pl.BlockSpec(memory_space=pl.ANY),
                  pl.BlockSpec((MOE_TILE, d), lambda i: (i, 0)),
                  pl.BlockSpec((1, d), lambda i: (0, 0)),
                  pl.BlockSpec(memory_space=pl.ANY)],
        out_specs=pl.BlockSpec(memory_space=pl.ANY),
        scratch_shapes=[pltpu.SMEM((TOP_K * MOE_TILE,), I32), pltpu.VMEM((MOE_TILE, HALF_D), U32),
                        pltpu.SemaphoreType.DMA, pltpu.SemaphoreType.DMA],
        input_output_aliases={3: 0},
        compiler_params=_params(("arbitrary",)),
        name="moe_dispatch",
    )(dest.reshape(ntiles, TOP_K * MOE_TILE), h, g.reshape(1, d), zeros)


def _moe_ffn_kernel(blk_exp_ref, nblk_ref, xs_ref, wg_ref, wu_ref, wd_ref, out_ref, x_scr, acc_scr):
    del blk_exp_ref
    i = pl.program_id(0)
    f = pl.program_id(1)
    live = i < nblk_ref[0]

    @pl.when(jnp.logical_and(live, f == 0))
    def _():
        x_scr[...] = _unpack_bf16_pairs(xs_ref[...])
        acc_scr[...] = jnp.zeros_like(acc_scr)

    @pl.when(live)
    def _():
        x = x_scr[...]
        a = _silu_mul(jnp.dot(x, wg_ref[0], preferred_element_type=F32),
                      jnp.dot(x, wu_ref[0], preferred_element_type=F32)).astype(BF16)
        acc_scr[...] += jnp.dot(a, wd_ref[0], preferred_element_type=F32)

    @pl.when(jnp.logical_and(live, f == pl.num_programs(1) - 1))
    def _():
        out_ref[...] = acc_scr[...]

    @pl.when(jnp.logical_and(jnp.logical_not(live), f == pl.num_programs(1) - 1))
    def _():
        out_ref[...] = jnp.zeros_like(out_ref)


def moe_ffn(xs, blk_exp, n_blocks, w_gate, w_up, w_down):
    n_rows = xs.shape[0]
    nb_max = n_rows // MOE_TILE
    nf = FFN_DIM // FFN_CHUNK

    def rows(i, f, be, nb):
        return (jnp.minimum(i, nb[0] - 1), 0)

    def fidx(i, f, nb):
        return jnp.where(i < nb[0], f, nf - 1)

    grid_spec = pltpu.PrefetchScalarGridSpec(
        num_scalar_prefetch=2,
        grid=(nb_max, nf),
        in_specs=[
            pl.BlockSpec((MOE_TILE, HALF_D), rows),
            pl.BlockSpec((1, D_MODEL, FFN_CHUNK), lambda i, f, be, nb: (be[i], 0, fidx(i, f, nb))),
            pl.BlockSpec((1, D_MODEL, FFN_CHUNK), lambda i, f, be, nb: (be[i], 0, fidx(i, f, nb))),
            pl.BlockSpec((1, FFN_CHUNK, D_MODEL), lambda i, f, be, nb: (be[i], fidx(i, f, nb), 0)),
        ],
        out_specs=pl.BlockSpec((MOE_TILE, D_MODEL), lambda i, f, be, nb: (i, 0)),
        scratch_shapes=[pltpu.VMEM((MOE_TILE, D_MODEL), BF16), pltpu.VMEM((MOE_TILE, D_MODEL), F32)],
    )
    return pl.pallas_call(
        _moe_ffn_kernel,
        out_shape=jax.ShapeDtypeStruct((n_rows, D_MODEL), F32),
        grid_spec=grid_spec,
        compiler_params=_params(("arbitrary", "arbitrary")),
        name="moe_ffn",
    )(blk_exp, n_blocks, xs, w_gate, w_up, w_down)


def _combine_kernel(dest_hbm, h_ref, route_ref, y_hbm, out_ref, idx_smem, rows_scr, sem_idx, sem_rows):
    i = pl.program_id(0)
    cp = pltpu.make_async_copy(dest_hbm.at[i], idx_smem, sem_idx)
    cp.start()
    cp.wait()

    def row_copy(t, k):
        d = idx_smem[TOP_K * t + k]
        return pltpu.make_async_copy(y_hbm.at[pl.ds(d, 1), :], rows_scr.at[k, pl.ds(t, 1), :], sem_rows)

    def issue(t, c):
        for k in range(TOP_K):
            row_copy(t, k).start()
        return c

    lax.fori_loop(0, MOE_TILE, issue, 0)

    def drain(t, c):
        for k in range(TOP_K):
            row_copy(t, k).wait()
        return c

    lax.fori_loop(0, MOE_TILE, drain, 0)
    route = route_ref[...]
    g0 = route[:, 2:3]
    g1 = route[:, 3:4]
    out_ref[...] = h_ref[...] + (rows_scr[0] * g0 + rows_scr[1] * g1)


def moe_combine(h, route, y, dest):
    nt, d = h.shape
    ntiles = nt // MOE_TILE
    row = pl.BlockSpec((MOE_TILE, d), lambda i: (i, 0))
    return pl.pallas_call(
        _combine_kernel,
        out_shape=jax.ShapeDtypeStruct((nt, d), F32),
        grid=(ntiles,),
        in_specs=[pl.BlockSpec(memory_space=pl.ANY), row,
                  pl.BlockSpec((MOE_TILE, LANES), lambda i: (i, 0)),
                  pl.BlockSpec(memory_space=pl.ANY)],
        out_specs=row,
        scratch_shapes=[pltpu.SMEM((TOP_K * MOE_TILE,), I32), pltpu.VMEM((TOP_K, MOE_TILE, d), F32),
                        pltpu.SemaphoreType.DMA, pltpu.SemaphoreType.DMA],
        compiler_params=_params(("arbitrary",)),
        name="moe_combine",
    )(dest.reshape(ntiles, TOP_K * MOE_TILE), h, route, y)


def _moe_plan(route, nt):
    experts = route[:, :TOP_K].astype(I32).reshape(-1)
    onehot = (experts[:, None] == jnp.arange(N_EXPERTS, dtype=I32)[None, :]).astype(I32)
    before = jnp.cumsum(onehot, axis=0) - onehot
    rank = jnp.sum(before * onehot, axis=1)
    counts = jnp.sum(onehot, axis=0)
    padded = (counts + MOE_TILE - 1) // MOE_TILE * MOE_TILE
    pend = jnp.cumsum(padded)
    pstart = pend - padded
    dest = jnp.sum(onehot * pstart[None, :], axis=1) + rank
    nb_max = -(-(nt * TOP_K + N_EXPERTS * (MOE_TILE - 1)) // MOE_TILE)
    blk_start = jnp.arange(nb_max, dtype=I32) * MOE_TILE
    blk_exp = jnp.minimum(jnp.sum((blk_start[:, None] >= pend[None, :]).astype(I32), axis=1), N_EXPERTS - 1)
    n_blocks = (pend[-1] // MOE_TILE).astype(I32).reshape(1)
    last_exp = jnp.max(jnp.where(counts > 0, jnp.arange(N_EXPERTS, dtype=I32), 0))
    blk_exp = jnp.where(blk_start < pend[-1], blk_exp, last_exp)
    return dest.astype(I32), blk_exp.astype(I32), n_blocks, nb_max * MOE_TILE


def _moe_layer(h, route, g, w_gate, w_up, w_down):
    nt = h.shape[0]
    dest, blk_exp, n_blocks, n_rows = _moe_plan(route, nt)
    xs = moe_dispatch(h, g, dest, n_rows)
    y = moe_ffn(xs, blk_exp, n_blocks, w_gate, w_up, w_down)
    return moe_combine(h, route, y, dest)


def _forward(x, meta_tokens, t5_bias, layers):
    batch = x.shape[0]
    h = jnp.concatenate([x.reshape(batch * SEQ, D_MODEL).astype(F32),
                         jnp.tile(meta_tokens.astype(F32), (batch, 1))], axis=0)
    for i, p in enumerate(layers):
        wqkv = p['w_qkv'].astype(BF16)
        if i % 2 == 0:
            qkv = norm_matmul(h, p['norm_mix'], wqkv, 512)
            o_real, o_meta = na_attention(qkv, p['q_norm'], p['k_norm'], p['rpb'], batch)
        else:
            qkv = norm_matmul(h, p['norm_mix'], wqkv, 512)
            o_real, o_meta = swa_attention(qkv, p['q_norm'], p['k_norm'], p['sink'], t5_bias, batch)
        o = jnp.concatenate([o_real, o_meta], axis=0)
        w_o = p['w_o'].astype(BF16)
        wg, wu, wd = (p[k].astype(BF16) for k in ('w_gate', 'w_up', 'w_down'))
        if i % 2 == 0:
            h = attn_out(o, w_o, h)
            h = dense_ffn(h, p['norm_ffn'], wg, wu, wd)
        else:
            h, route = attn_out_router(o, w_o, h, p['norm_ffn'], p['w_router'])
            h = _moe_layer(h, route, p['norm_ffn'], wg, wu, wd)
    return h[:batch * SEQ].reshape(batch, SEQ, D_MODEL)


def kernel(x, meta_tokens, t5_bias, l0_norm_mix, l0_w_qkv, l0_q_norm, l0_k_norm, l0_rpb, l0_w_o, l0_norm_ffn, l0_w_gate, l0_w_up, l0_w_down, l1_norm_mix, l1_w_qkv, l1_q_norm, l1_k_norm, l1_sink, l1_w_o, l1_norm_ffn, l1_w_router, l1_w_gate, l1_w_up, l1_w_down, l2_norm_mix, l2_w_qkv, l2_q_norm, l2_k_norm, l2_rpb, l2_w_o, l2_norm_ffn, l2_w_gate, l2_w_up, l2_w_down, l3_norm_mix, l3_w_qkv, l3_q_norm, l3_k_norm, l3_sink, l3_w_o, l3_norm_ffn, l3_w_router, l3_w_gate, l3_w_up, l3_w_down):
    layers = [
        dict(norm_mix=l0_norm_mix, w_qkv=l0_w_qkv, q_norm=l0_q_norm, k_norm=l0_k_norm, rpb=l0_rpb, w_o=l0_w_o,
             norm_ffn=l0_norm_ffn, w_gate=l0_w_gate, w_up=l0_w_up, w_down=l0_w_down),
        dict(norm_mix=l1_norm_mix, w_qkv=l1_w_qkv, q_norm=l1_q_norm, k_norm=l1_k_norm, sink=l1_sink, w_o=l1_w_o,
             norm_ffn=l1_norm_ffn, w_router=l1_w_router, w_gate=l1_w_gate, w_up=l1_w_up, w_down=l1_w_down),
        dict(norm_mix=l2_norm_mix, w_qkv=l2_w_qkv, q_norm=l2_q_norm, k_norm=l2_k_norm, rpb=l2_rpb, w_o=l2_w_o,
             norm_ffn=l2_norm_ffn, w_gate=l2_w_gate, w_up=l2_w_up, w_down=l2_w_down),
        dict(norm_mix=l3_norm_mix, w_qkv=l3_w_qkv, q_norm=l3_q_norm, k_norm=l3_k_norm, sink=l3_sink, w_o=l3_w_o,
             norm_ffn=l3_norm_ffn, w_router=l3_w_router, w_gate=l3_w_gate, w_up=l3_w_up, w_down=l3_w_down),
    ]
    return _forward(x, meta_tokens, t5_bias, layers)
```

```python
import functools
import math

import jax
import jax.numpy as jnp
import numpy as np
from jax import lax
from jax.experimental import pallas as pl
from jax.experimental.pallas import tpu as pltpu

D_MODEL = 1024
SEQ = 2048
N_META = 16
GRID_W = 64
NA_HEADS = 16
NA_KH = 8
NA_KW = 16
HEAD_DIM = 64
SWA_Q_HEADS = 16
SWA_KV_HEADS = 4
SWA_GROUP = SWA_Q_HEADS // SWA_KV_HEADS
SWA_WINDOW = 128
SWA_BLOCK = 128
T5_BUCKETS = 32
T5_MAX_DIST = 128
FFN_DIM = 3584
N_EXPERTS = 8
TOP_K = 2
EPS = 1e-6
NEG_INF = -1e30

LANES = 128
ROW_TILE = 1536
FFN_CHUNK = 512
MOE_TILE = 512
QKV_ROW_CHUNK = 512
VMEM_LIMIT = 56 * 1024 * 1024

F32 = jnp.float32
BF16 = jnp.bfloat16
I32 = jnp.int32
U32 = jnp.uint32


def _params(sem, vmem=VMEM_LIMIT):
    return pltpu.CompilerParams(dimension_semantics=sem, vmem_limit_bytes=vmem)


def _rms_rows(x, g):
    return (x * lax.rsqrt(jnp.mean(x * x, axis=-1, keepdims=True) + EPS)) * g


def _pair_norm(x, g, lo):
    sq = x * x
    s_lo = jnp.sum(jnp.where(lo, sq, 0.0), axis=-1, keepdims=True)
    s_hi = jnp.sum(jnp.where(lo, 0.0, sq), axis=-1, keepdims=True)
    ms = jnp.where(lo, s_lo, s_hi) * (1.0 / HEAD_DIM)
    return (x * lax.rsqrt(ms + EPS)) * g


def _qkv_kernel(h_ref, g_ref, w_ref, hg_ref, o_ref, y_scr, *, normed_tiles):
    j = pl.program_id(1)

    @pl.when(j == 0)
    def _():
        y_scr[...] = _rms_rows(h_ref[...], g_ref[...]).astype(BF16)

    def tile(normed):
        lo = lax.broadcasted_iota(I32, (1, LANES), 1) < HEAD_DIM
        for r0 in range(0, o_ref.shape[0], QKV_ROW_CHUNK):
            rows = slice(r0, r0 + QKV_ROW_CHUNK)
            acc = jnp.dot(y_scr[rows, :], w_ref[...], preferred_element_type=F32)
            if normed:
                for c in range(acc.shape[1] // LANES):
                    cs = slice(c * LANES, (c + 1) * LANES)
                    o_ref[rows, cs] = _pair_norm(acc[:, cs], hg_ref[:, cs], lo).astype(o_ref.dtype)
            else:
                o_ref[rows, :] = acc.astype(o_ref.dtype)

    @pl.when(j < normed_tiles)
    def _():
        tile(True)

    @pl.when(j >= normed_tiles)
    def _():
        tile(False)


def qkv_projection(h, g, w, q_g, k_g, n_q_heads, n_k_heads, tn):
    nt, d = h.shape
    nout = w.shape[1]
    n_qk = (n_q_heads + n_k_heads) * HEAD_DIM
    head_gain = jnp.concatenate([jnp.tile(q_g.astype(F32) * HEAD_DIM ** -0.5, n_q_heads),
                                 jnp.tile(k_g.astype(F32), n_k_heads), jnp.ones((nout - n_qk,), F32)])
    return pl.pallas_call(
        functools.partial(_qkv_kernel, normed_tiles=n_qk // tn),
        out_shape=jax.ShapeDtypeStruct((nt, nout), BF16),
        grid=(nt // ROW_TILE, nout // tn),
        in_specs=[
            pl.BlockSpec((ROW_TILE, d), lambda i, j: (i, 0)),
            pl.BlockSpec((1, d), lambda i, j: (0, 0)),
            pl.BlockSpec((d, tn), lambda i, j: (0, j)),
            pl.BlockSpec((1, tn), lambda i, j: (0, j)),
        ],
        out_specs=pl.BlockSpec((ROW_TILE, tn), lambda i, j: (i, j)),
        scratch_shapes=[pltpu.VMEM((ROW_TILE, d), BF16)],
        compiler_params=_params(("parallel", "arbitrary")),
        name="qkv_projection",
    )(h, g.reshape(1, d), w, head_gain.reshape(1, nout))


def _dot_t(a, b):
    return lax.dot_general(a, b, (((1,), (1,)), ((), ())), preferred_element_type=F32)


NA_ROWS = SEQ // GRID_W
NA_NK = NA_KH * GRID_W
NA_VARIANTS = 8


NA_UNROLL = 8


def _softmax_pv_group(items, kmp, vmp):
    scores = [jnp.concatenate([_dot_t(lhs, kw) + bias, _dot_t(lhs, kmp) + mb], axis=1)
              for lhs, kw, _, bias, mb in items]
    probs = []
    for s in scores:
        e = jnp.exp(s - jnp.max(s, axis=-1, keepdims=True))
        probs.append((e.astype(BF16), jnp.sum(e, axis=-1, keepdims=True)))
    outs = []
    for (eb, den), (_, kw, vw, _, _) in zip(probs, items):
        nk = kw.shape[0]
        o = jnp.dot(eb[:, :nk], vw, preferred_element_type=F32) + jnp.dot(eb[:, nk:], vmp, preferred_element_type=F32)
        outs.append(o / den)
    return outs


def _meta_only(lhs, kmp, vmp, meta_bias):
    s = _dot_t(lhs, kmp) + meta_bias
    e = jnp.exp(s - jnp.max(s, axis=-1, keepdims=True))
    return jnp.dot(e.astype(BF16), vmp, preferred_element_type=F32) / jnp.sum(e, axis=-1, keepdims=True)


def _split_pair(q, lo):
    zero = jnp.zeros((), q.dtype)
    return jnp.concatenate([jnp.where(lo, q, zero), jnp.where(lo, zero, q)], axis=0)


def _na_kernel(q_ref, k_ref, v_ref, qm_ref, km_ref, vm_ref, bias_ref, o_ref, om_ref, kmp_scr, vmp_scr):
    lane = lax.broadcasted_iota(I32, (1, LANES), 1)
    lo = lane < HEAD_DIM
    kmp_scr[...] = jnp.zeros_like(kmp_scr)
    kmp_scr[:N_META] = km_ref[...]
    vmp_scr[...] = jnp.zeros_like(vmp_scr)
    vmp_scr[:N_META] = vm_ref[...]
    kmp = kmp_scr[...]
    vmp = vmp_scr[...]
    meta_bias = jnp.where(lane < N_META, 0.0, NEG_INF)

    omm = _meta_only(_split_pair(qm_ref[...], lo), kmp, vmp, meta_bias)
    om_ref[...] = jnp.where(lo, omm[:N_META], omm[N_META:]).astype(om_ref.dtype)

    def rows(t, carry):
        items, starts = [], []
        for u in range(NA_UNROLL):
            r = t * NA_UNROLL + u
            rs = jnp.clip(r - NA_KH // 2, 0, NA_ROWS - NA_KH)
            var = jnp.where(r < 4, r, jnp.where(r > NA_ROWS - 4, r - (NA_ROWS - NA_VARIANTS), 4))
            q0 = pl.multiple_of(r * GRID_W, GRID_W)
            k0 = pl.multiple_of(rs * GRID_W, GRID_W)
            lhs = _split_pair(q_ref[pl.ds(q0, GRID_W), :], lo)
            items.append((lhs, k_ref[pl.ds(k0, NA_NK), :], v_ref[pl.ds(k0, NA_NK), :], bias_ref[0, var], meta_bias))
            starts.append(q0)
        for q0, o in zip(starts, _softmax_pv_group(items, kmp, vmp)):
            o_ref[pl.ds(q0, GRID_W), :] = jnp.where(lo, o[:GRID_W], o[GRID_W:]).astype(o_ref.dtype)
        return carry

    lax.fori_loop(0, NA_ROWS // NA_UNROLL, rows, 0)


def _na_bias_tables(rpb):
    qc = np.arange(GRID_W)[:, None]
    kc = np.arange(GRID_W)[None, :]
    w_start = np.clip(qc - NA_KW // 2, 0, GRID_W - NA_KW)
    ok = (kc >= w_start) & (kc < w_start + NA_KW)
    dc = np.clip(kc - qc + NA_KW - 1, 0, 2 * NA_KW - 2)
    onehot = (dc[None] == np.arange(2 * NA_KW - 1)[:, None, None]) & ok[None]
    onehot = jnp.asarray(onehot.reshape(2 * NA_KW - 1, -1), F32)
    toe = jnp.dot(rpb.reshape(-1, 2 * NA_KW - 1).astype(F32), onehot, precision=lax.Precision.HIGHEST)
    toe = toe.reshape(NA_HEADS, 2 * NA_KH - 1, GRID_W, GRID_W)
    toe = jnp.where(jnp.asarray(ok)[None, None], toe, NEG_INF)
    reps = [0, 1, 2, 3, NA_ROWS // 2, NA_ROWS - 3, NA_ROWS - 2, NA_ROWS - 1]
    tabs = []
    for r in reps:
        rs = int(np.clip(r - NA_KH // 2, 0, NA_ROWS - NA_KH))
        tabs.append(jnp.concatenate([toe[:, rs - r + i + NA_KH - 1] for i in range(NA_KH)], axis=-1))
    t = jnp.stack(tabs, axis=1)
    t = t.reshape(NA_HEADS // 2, 2, NA_VARIANTS, GRID_W, NA_NK).transpose(0, 2, 1, 3, 4)
    return t.reshape(NA_HEADS // 2, NA_VARIANTS, 2 * GRID_W, NA_NK)


def na_attention(qkv, rpb, batch):
    npairs = NA_HEADS // 2
    mrow = batch * SEQ // N_META
    bias = _na_bias_tables(rpb)
    real = lambda off: pl.BlockSpec((SEQ, LANES), lambda j, b: (b, off + j))
    meta = lambda off: pl.BlockSpec((N_META, LANES), lambda j, b: (mrow + b, off + j))
    return pl.pallas_call(
        _na_kernel,
        out_shape=(jax.ShapeDtypeStruct((batch * SEQ, D_MODEL), BF16),
                   jax.ShapeDtypeStruct((batch * N_META, D_MODEL), BF16)),
        grid=(npairs, batch),
        in_specs=[real(0), real(npairs), real(2 * npairs), meta(0), meta(npairs), meta(2 * npairs),
                  pl.BlockSpec((1, NA_VARIANTS, 2 * GRID_W, NA_NK), lambda j, b: (j, 0, 0, 0))],
        out_specs=(pl.BlockSpec((SEQ, LANES), lambda j, b: (b, j)),
                   pl.BlockSpec((N_META, LANES), lambda j, b: (b, j))),
        scratch_shapes=[pltpu.VMEM((LANES, LANES), BF16), pltpu.VMEM((LANES, LANES), BF16)],
        compiler_params=_params(("parallel", "parallel")),
        name="na_attention",
    )(qkv, qkv, qkv, qkv, qkv, qkv, bias)


SWA_NB = SEQ // SWA_BLOCK
SWA_NK = 3 * SWA_BLOCK
SWA_ROWS = SWA_GROUP * SWA_BLOCK
SWA_MROWS = SWA_GROUP * N_META


SWA_UNROLL = 4


def _swa_kernel(q_ref, k_ref, v_ref, qm_ref, km_ref, vm_ref, bw_ref, bm_ref, bmm_ref,
                o_ref, om_ref, kd_scr, vd_scr, kmp_scr, vmp_scr):
    lane = lax.broadcasted_iota(I32, (1, LANES), 1)
    lo = lane < HEAD_DIM
    src = lax.broadcasted_iota(I32, (LANES, LANES), 0)
    dst = lax.broadcasted_iota(I32, (LANES, LANES), 1)
    spread = (src == dst % HEAD_DIM + HEAD_DIM * (pl.program_id(0) % 2)).astype(BF16)

    def dup(x):
        return jnp.dot(x, spread, preferred_element_type=F32).astype(BF16)

    kd_scr[...] = dup(k_ref[...])
    vd_scr[...] = dup(v_ref[...])
    kmp_scr[...] = jnp.zeros_like(kmp_scr)
    kmp_scr[:N_META] = dup(km_ref[...])
    vmp_scr[...] = jnp.zeros_like(vmp_scr)
    vmp_scr[:N_META] = dup(vm_ref[...])
    kmp = kmp_scr[...]
    vmp = vmp_scr[...]

    def stack(q):
        return jnp.concatenate([_split_pair(q[:, :LANES], lo), _split_pair(q[:, LANES:], lo)], axis=0)

    def unstack(o, t):
        return jnp.concatenate([jnp.where(lo, o[:t], o[t:2 * t]), jnp.where(lo, o[2 * t:3 * t], o[3 * t:])], axis=1)

    om_ref[...] = unstack(_meta_only(stack(qm_ref[...]), kmp, vmp, bmm_ref[0]), N_META).astype(om_ref.dtype)

    def blocks(t, carry):
        items, starts = [], []
        for u in range(SWA_UNROLL):
            i = t * SWA_UNROLL + u
            ws = jnp.clip(i - 1, 0, SWA_NB - 3)
            var = jnp.where(i == 0, 0, jnp.where(i == SWA_NB - 1, 2, 1))
            q0 = pl.multiple_of(i * SWA_BLOCK, SWA_BLOCK)
            k0 = pl.multiple_of(ws * SWA_BLOCK, SWA_BLOCK)
            lhs = stack(q_ref[pl.ds(q0, SWA_BLOCK), :])
            items.append((lhs, kd_scr[pl.ds(k0, SWA_NK), :], vd_scr[pl.ds(k0, SWA_NK), :], bw_ref[0, var],
                          bm_ref[0, jnp.minimum(i, 1)]))
            starts.append(q0)
        for q0, o in zip(starts, _softmax_pv_group(items, kmp, vmp)):
            o_ref[pl.ds(q0, SWA_BLOCK), :] = unstack(o, SWA_BLOCK).astype(o_ref.dtype)
        return carry

    lax.fori_loop(0, SWA_NB // SWA_UNROLL, blocks, 0)


def _t5_bucket(rel):
    nb = T5_BUCKETS // 2
    max_exact = nb // 2
    ret = jnp.where(rel > 0, nb, 0)
    n = jnp.abs(rel)
    nf = jnp.maximum(n, max_exact).astype(jnp.float32)
    large = max_exact + (jnp.log(nf / max_exact) / math.log(T5_MAX_DIST / max_exact) * (nb - max_exact)).astype(I32)
    large = jnp.minimum(large, nb - 1)
    return ret + jnp.where(n < max_exact, n, large)


def _swa_bias_tables(t5_bias, sink):
    tb = t5_bias.astype(F32)
    sink_hg = sink.astype(F32).reshape(SWA_KV_HEADS, SWA_GROUP)

    def meta_tile(b):
        q = b.shape[1] // SWA_GROUP
        s = jnp.repeat(sink_hg, q, axis=1)[..., None]
        pad = jnp.full(b.shape[:2] + (LANES - N_META - 1,), NEG_INF, F32)
        return jnp.concatenate([b, s, pad], axis=-1)

    def head_bias(rel):
        b = jnp.moveaxis(tb[_t5_bucket(rel)], -1, 0)
        return b.reshape(SWA_KV_HEADS, SWA_GROUP * rel.shape[0], rel.shape[1])

    qi = jnp.arange(SWA_BLOCK)[:, None]
    j = jnp.arange(SWA_NK)[None, :]
    wins = []
    for off in (0, SWA_BLOCK, 2 * SWA_BLOCK):
        rel = j - off - qi
        band = jnp.tile(jnp.abs(rel) <= SWA_WINDOW, (SWA_GROUP, 1))[None]
        wins.append(jnp.where(band, head_bias(rel), NEG_INF))
    bw = jnp.stack(wins, axis=1)
    pos_m = jnp.arange(N_META)[None, :]
    bm = jnp.stack([meta_tile(head_bias(pos_m - (N_META + blk * SWA_BLOCK + qi))) for blk in (0, 1)], axis=1)
    bmm = meta_tile(head_bias(pos_m - jnp.arange(N_META)[:, None]))
    return bw, bm, bmm


def swa_attention(qkv, sink, t5_bias, batch):
    nq = SWA_Q_HEADS * HEAD_DIM
    mrow = batch * SEQ // N_META
    bw, bm, bmm = _swa_bias_tables(t5_bias, sink)
    kcol = nq // LANES
    vcol = kcol + SWA_KV_HEADS * HEAD_DIM // LANES
    return pl.pallas_call(
        _swa_kernel,
        out_shape=(jax.ShapeDtypeStruct((batch * SEQ, D_MODEL), BF16),
                   jax.ShapeDtypeStruct((batch * N_META, D_MODEL), BF16)),
        grid=(SWA_KV_HEADS, batch),
        in_specs=[
            pl.BlockSpec((SEQ, 2 * LANES), lambda n, b: (b, n)),
            pl.BlockSpec((SEQ, LANES), lambda n, b: (b, kcol + n // 2)),
            pl.BlockSpec((SEQ, LANES), lambda n, b: (b, vcol + n // 2)),
            pl.BlockSpec((N_META, 2 * LANES), lambda n, b: (mrow + b, n)),
            pl.BlockSpec((N_META, LANES), lambda n, b: (mrow + b, kcol + n // 2)),
            pl.BlockSpec((N_META, LANES), lambda n, b: (mrow + b, vcol + n // 2)),
            pl.BlockSpec((1, 3, SWA_ROWS, SWA_NK), lambda n, b: (n, 0, 0, 0)),
            pl.BlockSpec((1, 2, SWA_ROWS, LANES), lambda n, b: (n, 0, 0, 0)),
            pl.BlockSpec((1, SWA_MROWS, LANES), lambda n, b: (n, 0, 0)),
        ],
        out_specs=(pl.BlockSpec((SEQ, 2 * LANES), lambda n, b: (b, n)),
                   pl.BlockSpec((N_META, 2 * LANES), lambda n, b: (b, n))),
        scratch_shapes=[pltpu.VMEM((SEQ, LANES), BF16), pltpu.VMEM((SEQ, LANES), BF16),
                        pltpu.VMEM((LANES, LANES), BF16), pltpu.VMEM((LANES, LANES), BF16)],
        compiler_params=_params(("parallel", "parallel")),
        name="swa_attention",
    )(qkv, qkv, qkv, qkv, qkv, qkv, bw, bm, bmm)


def _attn_out_kernel(o_ref, w_ref, h_ref, out_ref):
    out_ref[...] = h_ref[...] + jnp.dot(o_ref[...], w_ref[...], preferred_element_type=F32)


def attn_out(o, w_o, h):
    nt, d = h.shape
    row = pl.BlockSpec((ROW_TILE, d), lambda i: (i, 0))
    return pl.pallas_call(
        _attn_out_kernel,
        out_shape=jax.ShapeDtypeStruct((nt, d), F32),
        grid=(nt // ROW_TILE,),
        in_specs=[row, pl.BlockSpec((d, d), lambda i: (0, 0)), row],
        out_specs=row,
        compiler_params=_params(("parallel",)),
        name="attn_out",
    )(o, w_o, h)


def _attn_out_router_kernel(o_ref, w_ref, h_ref, g_ref, wr_ref, out_ref, route_ref):
    hn = h_ref[...] + jnp.dot(o_ref[...], w_ref[...], preferred_element_type=F32)
    out_ref[...] = hn
    y = _rms_rows(hn, g_ref[...]).astype(BF16)
    lane = lax.broadcasted_iota(I32, (1, LANES), 1).astype(F32)
    logits = jnp.dot(y, wr_ref[...], preferred_element_type=F32)
    logits = jnp.where(lane < N_EXPERTS, logits, -jnp.inf)
    v1 = jnp.max(logits, axis=-1, keepdims=True)
    i1 = jnp.min(jnp.where(logits == v1, lane, float(LANES)), axis=-1, keepdims=True)
    rest = jnp.where(lane == i1, -jnp.inf, logits)
    v2 = jnp.max(rest, axis=-1, keepdims=True)
    i2 = jnp.min(jnp.where(rest == v2, lane, float(LANES)), axis=-1, keepdims=True)
    e2 = jnp.exp(v2 - v1)
    den = 1.0 + e2
    route = jnp.where(lane == 0, i1,
                      jnp.where(lane == 1, i2,
                                jnp.where(lane == 2, 1.0 / den, jnp.where(lane == 3, e2 / den, 0.0))))
    route_ref[...] = route


def attn_out_router(o, w_o, h, g, w_router):
    nt, d = h.shape
    wr = jnp.zeros((d, LANES), BF16).at[:, :N_EXPERTS].set(w_router.astype(BF16))
    row = pl.BlockSpec((ROW_TILE, d), lambda i: (i, 0))
    return pl.pallas_call(
        _attn_out_router_kernel,
        out_shape=(jax.ShapeDtypeStruct((nt, d), F32), jax.ShapeDtypeStruct((nt, LANES), F32)),
        grid=(nt // ROW_TILE,),
        in_specs=[row, pl.BlockSpec((d, d), lambda i: (0, 0)), row, pl.BlockSpec((1, d), lambda i: (0, 0)),
                  pl.BlockSpec((d, LANES), lambda i: (0, 0))],
        out_specs=(row, pl.BlockSpec((ROW_TILE, LANES), lambda i: (i, 0))),
        compiler_params=_params(("parallel",)),
        name="attn_out_router",
    )(o, w_o, h, g.reshape(1, d), wr)


def _silu_mul(g, u):
    return (g / (1.0 + jnp.exp(-g))) * u


def _dense_ffn_kernel(h_ref, g_ref, wg_ref, wu_ref, wd_ref, out_ref, y_scr, acc_scr):
    f = pl.program_id(1)

    @pl.when(f == 0)
    def _():
        y_scr[...] = _rms_rows(h_ref[...], g_ref[...]).astype(BF16)
        acc_scr[...] = jnp.zeros_like(acc_scr)

    y = y_scr[...]
    a = _silu_mul(jnp.dot(y, wg_ref[...], preferred_element_type=F32),
                  jnp.dot(y, wu_ref[...], preferred_element_type=F32)).astype(BF16)
    acc_scr[...] += jnp.dot(a, wd_ref[...], preferred_element_type=F32)

    @pl.when(f == pl.num_programs(1) - 1)
    def _():
        out_ref[...] = h_ref[...] + acc_scr[...]


def dense_ffn(h, g, w_gate, w_up, w_down):
    nt, d = h.shape
    row = pl.BlockSpec((ROW_TILE, d), lambda i, f: (i, 0))
    return pl.pallas_call(
        _dense_ffn_kernel,
        out_shape=jax.ShapeDtypeStruct((nt, d), F32),
        grid=(nt // ROW_TILE, FFN_DIM // FFN_CHUNK),
        in_specs=[row, pl.BlockSpec((1, d), lambda i, f: (0, 0)),
                  pl.BlockSpec((d, FFN_CHUNK), lambda i, f: (0, f)),
                  pl.BlockSpec((d, FFN_CHUNK), lambda i, f: (0, f)),
                  pl.BlockSpec((FFN_CHUNK, d), lambda i, f: (f, 0))],
        out_specs=row,
        scratch_shapes=[pltpu.VMEM((ROW_TILE, d), BF16), pltpu.VMEM((ROW_TILE, d), F32)],
        compiler_params=_params(("parallel", "arbitrary")),
        name="dense_ffn",
    )(h, g.reshape(1, d), w_gate, w_up, w_down)


HALF_D = D_MODEL // 2


def _pack_bf16_pairs(y):
    r = y.astype(BF16).astype(F32)
    hi = lax.bitcast_convert_type(r[:, :HALF_D], U32)
    low = lax.bitcast_convert_type(r[:, HALF_D:], U32)
    return hi | (low >> 16)


def _unpack_bf16_pairs(w):
    hi = lax.bitcast_convert_type(w & jnp.uint32(0xFFFF0000), F32)
    low = lax.bitcast_convert_type(w << 16, F32)
    return jnp.concatenate([hi, low], axis=1).astype(BF16)


def _dispatch_kernel(dest_hbm, h_ref, g_ref, zeros_hbm, xs_hbm, idx_smem, y_scr, sem_idx, sem_rows):
    del zeros_hbm
    i = pl.program_id(0)
    cp = pltpu.make_async_copy(dest_hbm.at[i], idx_smem, sem_idx)
    cp.start()
    y_scr[...] = _pack_bf16_pairs(_rms_rows(h_ref[...], g_ref[...]))
    cp.wait()

    def row_copy(t, k):
        d = idx_smem[TOP_K * t + k]
        return pltpu.make_async_copy(y_scr.at[pl.ds(t, 1), :], xs_hbm.at[pl.ds(d, 1), :], sem_rows)

    def issue(t, c):
        for k in range(TOP_K):
            row_copy(t, k).start()
        return c

    lax.fori_loop(0, MOE_TILE, issue, 0)

    def drain(t, c):
        for k in range(TOP_K):
            row_copy(t, k).wait()
        return c

    lax.fori_loop(0, MOE_TILE, drain, 0)


def moe_dispatch(h, g, dest, n_rows):
    nt, d = h.shape
    ntiles = nt // MOE_TILE
    zeros = jnp.zeros((n_rows, HALF_D), U32)
    return pl.pallas_call(
        _dispatch_kernel,
        out_shape=jax.ShapeDtypeStruct((n_rows, HALF_D), U32),
        grid=(ntiles,),
        in_specs=[pl.BlockSpec(memory_space=pl.ANY),
                  pl.BlockSpec((MOE_TILE, d), lambda i: (i, 0)),
                  pl.BlockSpec((1, d), lambda i: (0, 0)),
                  pl.BlockSpec(memory_space=pl.ANY)],
        out_specs=pl.BlockSpec(memory_space=pl.ANY),
        scratch_shapes=[pltpu.SMEM((TOP_K * MOE_TILE,), I32), pltpu.VMEM((MOE_TILE, HALF_D), U32),
                        pltpu.SemaphoreType.DMA, pltpu.SemaphoreType.DMA],
        input_output_aliases={3: 0},
        compiler_params=_params(("arbitrary",)),
        name="moe_dispatch",
    )(dest.reshape(ntiles, TOP_K * MOE_TILE), h, g.reshape(1, d), zeros)


def _moe_ffn_kernel(blk_exp_ref, nblk_ref, xs_ref, wg_ref, wu_ref, wd_ref, out_ref, x_scr, acc_scr):
    del blk_exp_ref
    i = pl.program_id(0)
    f = pl.program_id(1)
    live = i < nblk_ref[0]

    @pl.when(jnp.logical_and(live, f == 0))
    def _():
        x_scr[...] = _unpack_bf16_pairs(xs_ref[...])
        acc_scr[...] = jnp.zeros_like(acc_scr)

    @pl.when(live)
    def _():
        x = x_scr[...]
        a = _silu_mul(jnp.dot(x, wg_ref[0], preferred_element_type=F32),
                      jnp.dot(x, wu_ref[0], preferred_element_type=F32)).astype(BF16)
        acc_scr[...] += jnp.dot(a, wd_ref[0], preferred_element_type=F32)

    @pl.when(jnp.logical_and(live, f == pl.num_programs(1) - 1))
    def _():
        out_ref[...] = acc_scr[...]

    @pl.when(jnp.logical_and(jnp.logical_not(live), f == pl.num_programs(1) - 1))
    def _():
        out_ref[...] = jnp.zeros_like(out_ref)


def moe_ffn(xs, blk_exp, n_blocks, w_gate, w_up, w_down):
    n_rows = xs.shape[0]
    nb_max = n_rows // MOE_TILE
    nf = FFN_DIM // FFN_CHUNK

    def rows(i, f, be, nb):
        return (jnp.minimum(i, nb[0] - 1), 0)

    def fidx(i, f, nb):
        return jnp.where(i < nb[0], f, nf - 1)

    grid_spec = pltpu.PrefetchScalarGridSpec(
        num_scalar_prefetch=2,
        grid=(nb_max, nf),
        in_specs=[
            pl.BlockSpec((MOE_TILE, HALF_D), rows),
            pl.BlockSpec((1, D_MODEL, FFN_CHUNK), lambda i, f, be, nb: (be[i], 0, fidx(i, f, nb))),
            pl.BlockSpec((1, D_MODEL, FFN_CHUNK), lambda i, f, be, nb: (be[i], 0, fidx(i, f, nb))),
            pl.BlockSpec((1, FFN_CHUNK, D_MODEL), lambda i, f, be, nb: (be[i], fidx(i, f, nb), 0)),
        ],
        out_specs=pl.BlockSpec((MOE_TILE, D_MODEL), lambda i, f, be, nb: (i, 0)),
        scratch_shapes=[pltpu.VMEM((MOE_TILE, D_MODEL), BF16), pltpu.VMEM((MOE_TILE, D_MODEL), F32)],
    )
    return pl.pallas_call(
        _moe_ffn_kernel,
        out_shape=jax.ShapeDtypeStruct((n_rows, D_MODEL), F32),
        grid_spec=grid_spec,
        compiler_params=_params(("arbitrary", "arbitrary")),
        name="moe_ffn",
    )(blk_exp, n_blocks, xs, w_gate, w_up, w_down)


def _combine_kernel(dest_hbm, h_ref, route_ref, y_hbm, out_ref, idx_smem, rows_scr, sem_idx, sem_rows):
    i = pl.program_id(0)
    cp = pltpu.make_async_copy(dest_hbm.at[i], idx_smem, sem_idx)
    cp.start()
    cp.wait()

    def row_copy(t, k):
        d = idx_smem[TOP_K * t + k]
        return pltpu.make_async_copy(y_hbm.at[pl.ds(d, 1), :], rows_scr.at[k, pl.ds(t, 1), :], sem_rows)

    def issue(t, c):
        for k in range(TOP_K):
            row_copy(t, k).start()
        return c

    lax.fori_loop(0, MOE_TILE, issue, 0)

    def drain(t, c):
        for k in range(TOP_K):
            row_copy(t, k).wait()
        return c

    lax.fori_loop(0, MOE_TILE, drain, 0)
    route = route_ref[...]
    g0 = route[:, 2:3]
    g1 = route[:, 3:4]
    out_ref[...] = h_ref[...] + (rows_scr[0] * g0 + rows_scr[1] * g1)


def moe_combine(h, route, y, dest):
    nt, d = h.shape
    ntiles = nt // MOE_TILE
    row = pl.BlockSpec((MOE_TILE, d), lambda i: (i, 0))
    return pl.pallas_call(
        _combine_kernel,
        out_shape=jax.ShapeDtypeStruct((nt, d), F32),
        grid=(ntiles,),
        in_specs=[pl.BlockSpec(memory_space=pl.ANY), row,
                  pl.BlockSpec((MOE_TILE, LANES), lambda i: (i, 0)),
                  pl.BlockSpec(memory_space=pl.ANY)],
        out_specs=row,
        scratch_shapes=[pltpu.SMEM((TOP_K * MOE_TILE,), I32), pltpu.VMEM((TOP_K, MOE_TILE, d), F32),
                        pltpu.SemaphoreType.DMA, pltpu.SemaphoreType.DMA],
        compiler_params=_params(("arbitrary",)),
        name="moe_combine",
    )(dest.reshape(ntiles, TOP_K * MOE_TILE), h, route, y)


def _moe_plan(route, nt):
    experts = route[:, :TOP_K].astype(I32).reshape(-1)
    onehot = (experts[:, None] == jnp.arange(N_EXPERTS, dtype=I32)[None, :]).astype(I32)
    before = jnp.cumsum(onehot, axis=0) - onehot
    rank = jnp.sum(before * onehot, axis=1)
    counts = jnp.sum(onehot, axis=0)
    padded = (counts + MOE_TILE - 1) // MOE_TILE * MOE_TILE
    pend = jnp.cumsum(padded)
    pstart = pend - padded
    dest = jnp.sum(onehot * pstart[None, :], axis=1) + rank
    nb_max = -(-(nt * TOP_K + N_EXPERTS * (MOE_TILE - 1)) // MOE_TILE)
    blk_start = jnp.arange(nb_max, dtype=I32) * MOE_TILE
    blk_exp = jnp.minimum(jnp.sum((blk_start[:, None] >= pend[None, :]).astype(I32), axis=1), N_EXPERTS - 1)
    n_blocks = (pend[-1] // MOE_TILE).astype(I32).reshape(1)
    last_exp = jnp.max(jnp.where(counts > 0, jnp.arange(N_EXPERTS, dtype=I32), 0))
    blk_exp = jnp.where(blk_start < pend[-1], blk_exp, last_exp)
    return dest.astype(I32), blk_exp.astype(I32), n_blocks, nb_max * MOE_TILE


def _moe_layer(h, route, g, w_gate, w_up, w_down):
    nt = h.shape[0]
    dest, blk_exp, n_blocks, n_rows = _moe_plan(route, nt)
    xs = moe_dispatch(h, g, dest, n_rows)
    y = moe_ffn(xs, blk_exp, n_blocks, w_gate, w_up, w_down)
    return moe_combine(h, route, y, dest)


def _forward(x, meta_tokens, t5_bias, layers):
    batch = x.shape[0]
    h = jnp.concatenate([x.reshape(batch * SEQ, D_MODEL).astype(F32),
                         jnp.tile(meta_tokens.astype(F32), (batch, 1))], axis=0)
    for i, p in enumerate(layers):
        wqkv = p['w_qkv'].astype(BF16)
        if i % 2 == 0:
            qkv = qkv_projection(h, p['norm_mix'], wqkv, p['q_norm'], p['k_norm'], NA_HEADS, NA_HEADS, 512)
            o_real, o_meta = na_attention(qkv, p['rpb'], batch)
        else:
            qkv = qkv_projection(h, p['norm_mix'], wqkv, p['q_norm'], p['k_norm'], SWA_Q_HEADS, SWA_KV_HEADS, 256)
            o_real, o_meta = swa_attention(qkv, p['sink'], t5_bias, batch)
        o = jnp.concatenate([o_real, o_meta], axis=0)
        w_o = p['w_o'].astype(BF16)
        wg, wu, wd = (p[k].astype(BF16) for k in ('w_gate', 'w_up', 'w_down'))
        if i % 2 == 0:
            h = attn_out(o, w_o, h)
            h = dense_ffn(h, p['norm_ffn'], wg, wu, wd)
        else:
            h, route = attn_out_router(o, w_o, h, p['norm_ffn'], p['w_router'])
            h = _moe_layer(h, route, p['norm_ffn'], wg, wu, wd)
    return h[:batch * SEQ].reshape(batch, SEQ, D_MODEL)


def kernel(x, meta_tokens, t5_bias, l0_norm_mix, l0_w_qkv, l0_q_norm, l0_k_norm, l0_rpb, l0_w_o, l0_norm_ffn, l0_w_gate, l0_w_up, l0_w_down, l1_norm_mix, l1_w_qkv, l1_q_norm, l1_k_norm, l1_sink, l1_w_o, l1_norm_ffn, l1_w_router, l1_w_gate, l1_w_up, l1_w_down, l2_norm_mix, l2_w_qkv, l2_q_norm, l2_k_norm, l2_rpb, l2_w_o, l2_norm_ffn, l2_w_gate, l2_w_up, l2_w_down, l3_norm_mix, l3_w_qkv, l3_q_norm, l3_k_norm, l3_sink, l3_w_o, l3_norm_ffn, l3_w_router, l3_w_gate, l3_w_up, l3_w_down):
    layers = [
        dict(norm_mix=l0_norm_mix, w_qkv=l0_w_qkv, q_norm=l0_q_norm, k_norm=l0_k_norm, rpb=l0_rpb, w_o=l0_w_o,
             norm_ffn=l0_norm_ffn, w_gate=l0_w_gate, w_up=l0_w_up, w_down=l0_w_down),
        dict(norm_mix=l1_norm_mix, w_qkv=l1_w_qkv, q_norm=l1_q_norm, k_norm=l1_k_norm, sink=l1_sink, w_o=l1_w_o,
             norm_ffn=l1_norm_ffn, w_router=l1_w_router, w_gate=l1_w_gate, w_up=l1_w_up, w_down=l1_w_down),
        dict(norm_mix=l2_norm_mix, w_qkv=l2_w_qkv, q_norm=l2_q_norm, k_norm=l2_k_norm, rpb=l2_rpb, w_o=l2_w_o,
             norm_ffn=l2_norm_ffn, w_gate=l2_w_gate, w_up=l2_w_up, w_down=l2_w_down),
        dict(norm_mix=l3_norm_mix, w_qkv=l3_w_qkv, q_norm=l3_q_norm, k_norm=l3_k_norm, sink=l3_sink, w_o=l3_w_o,
             norm_ffn=l3_norm_ffn, w_router=l3_w_router, w_gate=l3_w_gate, w_up=l3_w_up, w_down=l3_w_down),
    ]
    return _forward(x, meta_tokens, t5_bias, layers)
```

```python
import functools
import math

import jax
import jax.numpy as jnp
import numpy as np
from jax import lax
from jax.experimental import pallas as pl
from jax.experimental.pallas import tpu as pltpu

D_MODEL = 1024
SEQ = 2048
N_META = 16
GRID_W = 64
NA_HEADS = 16
NA_KH = 8
NA_KW = 16
HEAD_DIM = 64
SWA_Q_HEADS = 16
SWA_KV_HEADS = 4
SWA_GROUP = SWA_Q_HEADS // SWA_KV_HEADS
SWA_WINDOW = 128
SWA_BLOCK = 128
T5_BUCKETS = 32
T5_MAX_DIST = 128
FFN_DIM = 3584
N_EXPERTS = 8
TOP_K = 2
EPS = 1e-6
NEG_INF = -1e30

LANES = 128
ROW_TILE = 1536
FFN_CHUNK = 512
MOE_TILE = 512
MOE_BLOCK = 1024
QKV_ROW_CHUNK = 512
VMEM_LIMIT = 56 * 1024 * 1024

F32 = jnp.float32
BF16 = jnp.bfloat16
I32 = jnp.int32
U32 = jnp.uint32


def _params(sem, vmem=VMEM_LIMIT):
    return pltpu.CompilerParams(dimension_semantics=sem, vmem_limit_bytes=vmem)


def _rms_rows(x, g):
    return (x * lax.rsqrt(jnp.mean(x * x, axis=-1, keepdims=True) + EPS)) * g


def _pair_norm(x, g, lo):
    sq = x * x
    s_lo = jnp.sum(jnp.where(lo, sq, 0.0), axis=-1, keepdims=True)
    s_hi = jnp.sum(jnp.where(lo, 0.0, sq), axis=-1, keepdims=True)
    ms = jnp.where(lo, s_lo, s_hi) * (1.0 / HEAD_DIM)
    return (x * lax.rsqrt(ms + EPS)) * g


def _qkv_kernel(h_ref, g_ref, w_ref, hg_ref, o_ref, y_scr, *, normed_cols, n_col_tiles):
    j = pl.program_id(1)
    tn = o_ref.shape[1]
    groups = tn // LANES
    lo = lax.broadcasted_iota(I32, (1, LANES), 1) < HEAD_DIM

    def tile(normed_groups, first):
        for r0 in range(0, o_ref.shape[0], QKV_ROW_CHUNK):
            rows = slice(r0, r0 + QKV_ROW_CHUNK)
            if first:
                y = _rms_rows(h_ref[rows, :], g_ref[...]).astype(BF16)
                y_scr[rows, :] = y
            else:
                y = y_scr[rows, :]
            acc = jnp.dot(y, w_ref[...], preferred_element_type=F32)
            for c in range(groups):
                cs = slice(c * LANES, (c + 1) * LANES)
                if c < normed_groups:
                    o_ref[rows, cs] = _pair_norm(acc[:, cs], hg_ref[:, cs], lo).astype(o_ref.dtype)
                else:
                    o_ref[rows, cs] = acc[:, cs].astype(o_ref.dtype)

    variants = {}
    for t in range(n_col_tiles):
        key = (min(max((normed_cols - t * tn) // LANES, 0), groups), t == 0)
        variants.setdefault(key, []).append(t)
    for (normed_groups, first), tiles in variants.items():
        cond = functools.reduce(jnp.logical_or, [j == t for t in tiles])
        pl.when(cond)(functools.partial(tile, normed_groups, first))


def qkv_projection(h, g, w, q_g, k_g, n_q_heads, n_k_heads, tn):
    nt, d = h.shape
    nout = w.shape[1]
    n_qk = (n_q_heads + n_k_heads) * HEAD_DIM
    head_gain = jnp.concatenate([jnp.tile(q_g.astype(F32) * HEAD_DIM ** -0.5, n_q_heads),
                                 jnp.tile(k_g.astype(F32), n_k_heads), jnp.ones((nout - n_qk,), F32)])
    return pl.pallas_call(
        functools.partial(_qkv_kernel, normed_cols=n_qk, n_col_tiles=nout // tn),
        out_shape=jax.ShapeDtypeStruct((nt, nout), BF16),
        grid=(nt // ROW_TILE, nout // tn),
        in_specs=[
            pl.BlockSpec((ROW_TILE, d), lambda i, j: (i, 0)),
            pl.BlockSpec((1, d), lambda i, j: (0, 0)),
            pl.BlockSpec((d, tn), lambda i, j: (0, j)),
            pl.BlockSpec((1, tn), lambda i, j: (0, j)),
        ],
        out_specs=pl.BlockSpec((ROW_TILE, tn), lambda i, j: (i, j)),
        scratch_shapes=[pltpu.VMEM((ROW_TILE, d), BF16)],
        compiler_params=_params(("parallel", "arbitrary")),
        name="qkv_projection",
    )(h, g.reshape(1, d), w, head_gain.reshape(1, nout))


def _dot_t(a, b):
    return lax.dot_general(a, b, (((1,), (1,)), ((), ())), preferred_element_type=F32)


NA_ROWS = SEQ // GRID_W
NA_NK = NA_KH * GRID_W
NA_VARIANTS = 8


NA_UNROLL = 8


def _softmax_pv_group(items, kmp, vmp):
    scores = [jnp.concatenate([_dot_t(lhs, kw) + bias, _dot_t(lhs, kmp) + mb], axis=1)
              for lhs, kw, _, bias, mb in items]
    probs = []
    for s in scores:
        e = jnp.exp(s - jnp.max(s, axis=-1, keepdims=True))
        probs.append((e.astype(BF16), jnp.sum(e, axis=-1, keepdims=True)))
    outs = []
    for (eb, den), (_, kw, vw, _, _) in zip(probs, items):
        nk = kw.shape[0]
        o = jnp.dot(eb[:, :nk], vw, preferred_element_type=F32) + jnp.dot(eb[:, nk:], vmp, preferred_element_type=F32)
        outs.append(o / den)
    return outs


def _meta_only(lhs, kmp, vmp, meta_bias):
    s = _dot_t(lhs, kmp) + meta_bias
    e = jnp.exp(s - jnp.max(s, axis=-1, keepdims=True))
    return jnp.dot(e.astype(BF16), vmp, preferred_element_type=F32) / jnp.sum(e, axis=-1, keepdims=True)


def _split_pair(q, lo):
    zero = jnp.zeros((), q.dtype)
    return jnp.concatenate([jnp.where(lo, q, zero), jnp.where(lo, zero, q)], axis=0)


def _na_meta_bias(lane):
    return jnp.where(lane < N_META, 0.0, NEG_INF)


def _na_kernel(q_ref, k_ref, v_ref, km_ref, vm_ref, bias_ref, o_ref, kmp_scr, vmp_scr):
    lane = lax.broadcasted_iota(I32, (1, LANES), 1)
    lo = lane < HEAD_DIM
    kmp_scr[...] = jnp.zeros_like(kmp_scr)
    kmp_scr[:N_META] = km_ref[...]
    vmp_scr[...] = jnp.zeros_like(vmp_scr)
    vmp_scr[:N_META] = vm_ref[...]
    kmp = kmp_scr[...]
    vmp = vmp_scr[...]
    meta_bias = _na_meta_bias(lane)

    def rows(t, carry):
        items, starts = [], []
        for u in range(NA_UNROLL):
            r = t * NA_UNROLL + u
            rs = jnp.clip(r - NA_KH // 2, 0, NA_ROWS - NA_KH)
            var = jnp.where(r < 4, r, jnp.where(r > NA_ROWS - 4, r - (NA_ROWS - NA_VARIANTS), 4))
            q0 = pl.multiple_of(r * GRID_W, GRID_W)
            k0 = pl.multiple_of(rs * GRID_W, GRID_W)
            lhs = _split_pair(q_ref[pl.ds(q0, GRID_W), :], lo)
            items.append((lhs, k_ref[pl.ds(k0, NA_NK), :], v_ref[pl.ds(k0, NA_NK), :], bias_ref[0, var], meta_bias))
            starts.append(q0)
        for q0, o in zip(starts, _softmax_pv_group(items, kmp, vmp)):
            o_ref[pl.ds(q0, GRID_W), :] = jnp.where(lo, o[:GRID_W], o[GRID_W:]).astype(o_ref.dtype)
        return carry

    lax.fori_loop(0, NA_ROWS // NA_UNROLL, rows, 0)


def _na_bias_tables(rpb):
    qc = np.arange(GRID_W)[:, None]
    kc = np.arange(GRID_W)[None, :]
    w_start = np.clip(qc - NA_KW // 2, 0, GRID_W - NA_KW)
    ok = (kc >= w_start) & (kc < w_start + NA_KW)
    dc = np.clip(kc - qc + NA_KW - 1, 0, 2 * NA_KW - 2)
    onehot = (dc[None] == np.arange(2 * NA_KW - 1)[:, None, None]) & ok[None]
    onehot = jnp.asarray(onehot.reshape(2 * NA_KW - 1, -1), F32)
    toe = jnp.dot(rpb.reshape(-1, 2 * NA_KW - 1).astype(F32), onehot, precision=lax.Precision.HIGHEST)
    toe = toe.reshape(NA_HEADS, 2 * NA_KH - 1, GRID_W, GRID_W)
    toe = jnp.where(jnp.asarray(ok)[None, None], toe, NEG_INF)
    reps = [0, 1, 2, 3, NA_ROWS // 2, NA_ROWS - 3, NA_ROWS - 2, NA_ROWS - 1]
    tabs = []
    for r in reps:
        rs = int(np.clip(r - NA_KH // 2, 0, NA_ROWS - NA_KH))
        tabs.append(jnp.concatenate([toe[:, rs - r + i + NA_KH - 1] for i in range(NA_KH)], axis=-1))
    t = jnp.stack(tabs, axis=1)
    t = t.reshape(NA_HEADS // 2, 2, NA_VARIANTS, GRID_W, NA_NK).transpose(0, 2, 1, 3, 4)
    return t.reshape(NA_HEADS // 2, NA_VARIANTS, 2 * GRID_W, NA_NK)


def na_attention(qkv, rpb, batch):
    npairs = NA_HEADS // 2
    mrow = batch * SEQ // N_META
    bias = _na_bias_tables(rpb)
    real = lambda off: pl.BlockSpec((SEQ, LANES), lambda j, b: (b, off + j))
    meta = lambda off: pl.BlockSpec((N_META, LANES), lambda j, b: (mrow + b, off + j))
    pad_tiles = [pltpu.VMEM((LANES, LANES), BF16), pltpu.VMEM((LANES, LANES), BF16)]
    o = pl.pallas_call(
        _na_kernel,
        out_shape=jax.ShapeDtypeStruct((qkv.shape[0], D_MODEL), BF16),
        grid=(npairs, batch),
        in_specs=[real(0), real(npairs), real(2 * npairs), meta(npairs), meta(2 * npairs),
                  pl.BlockSpec((1, NA_VARIANTS, 2 * GRID_W, NA_NK), lambda j, b: (j, 0, 0, 0))],
        out_specs=pl.BlockSpec((SEQ, LANES), lambda j, b: (b, j)),
        scratch_shapes=pad_tiles,
        compiler_params=_params(("parallel", "parallel")),
        name="na_attention",
    )(qkv, qkv, qkv, qkv, qkv, bias)
    return pl.pallas_call(
        _na_meta_kernel,
        out_shape=jax.ShapeDtypeStruct(o.shape, o.dtype),
        grid=(batch,),
        in_specs=[pl.BlockSpec((N_META, qkv.shape[1]), lambda b: (mrow + b, 0)), pl.BlockSpec(memory_space=pl.ANY)],
        out_specs=pl.BlockSpec((N_META, D_MODEL), lambda b: (mrow + b, 0)),
        scratch_shapes=pad_tiles,
        input_output_aliases={1: 0},
        compiler_params=_params(("parallel",)),
        name="na_meta_attention",
    )(qkv, o)


def _na_meta_kernel(qkv_ref, o_hbm, om_ref, kmp_scr, vmp_scr):
    del o_hbm
    lane = lax.broadcasted_iota(I32, (1, LANES), 1)
    lo = lane < HEAD_DIM
    kmp_scr[...] = jnp.zeros_like(kmp_scr)
    vmp_scr[...] = jnp.zeros_like(vmp_scr)
    hd = NA_HEADS * HEAD_DIM
    for j in range(NA_HEADS // 2):
        cs = slice(j * LANES, (j + 1) * LANES)
        kmp_scr[:N_META] = qkv_ref[:, hd + j * LANES:hd + (j + 1) * LANES]
        vmp_scr[:N_META] = qkv_ref[:, 2 * hd + j * LANES:2 * hd + (j + 1) * LANES]
        omm = _meta_only(_split_pair(qkv_ref[:, cs], lo), kmp_scr[...], vmp_scr[...], _na_meta_bias(lane))
        om_ref[:, cs] = jnp.where(lo, omm[:N_META], omm[N_META:]).astype(om_ref.dtype)


SWA_NB = SEQ // SWA_BLOCK
SWA_NK = 3 * SWA_BLOCK
SWA_ROWS = SWA_GROUP * SWA_BLOCK
SWA_MROWS = SWA_GROUP * N_META


SWA_UNROLL = 4


def _kv_spread(parity):
    src = lax.broadcasted_iota(I32, (LANES, LANES), 0)
    dst = lax.broadcasted_iota(I32, (LANES, LANES), 1)
    return (src == dst % HEAD_DIM + HEAD_DIM * parity).astype(BF16)


def _dup(x, spread):
    return jnp.dot(x, spread, preferred_element_type=F32).astype(BF16)


def _stack4(q, lo):
    return jnp.concatenate([_split_pair(q[:, :LANES], lo), _split_pair(q[:, LANES:], lo)], axis=0)


def _unstack4(o, t, lo):
    return jnp.concatenate([jnp.where(lo, o[:t], o[t:2 * t]), jnp.where(lo, o[2 * t:3 * t], o[3 * t:])], axis=1)


def _swa_kernel(q_ref, k_ref, v_ref, km_ref, vm_ref, bw_ref, bm_ref, o_ref, kd_scr, vd_scr, kmp_scr, vmp_scr):
    lane = lax.broadcasted_iota(I32, (1, LANES), 1)
    lo = lane < HEAD_DIM
    spread = _kv_spread(pl.program_id(0) % 2)
    kd_scr[...] = _dup(k_ref[...], spread)
    vd_scr[...] = _dup(v_ref[...], spread)
    kmp_scr[...] = jnp.zeros_like(kmp_scr)
    kmp_scr[:N_META] = _dup(km_ref[...], spread)
    vmp_scr[...] = jnp.zeros_like(vmp_scr)
    vmp_scr[:N_META] = _dup(vm_ref[...], spread)
    kmp = kmp_scr[...]
    vmp = vmp_scr[...]
    stack = lambda q: _stack4(q, lo)
    unstack = lambda o, t: _unstack4(o, t, lo)

    def blocks(t, carry):
        items, starts = [], []
        for u in range(SWA_UNROLL):
            i = t * SWA_UNROLL + u
            ws = jnp.clip(i - 1, 0, SWA_NB - 3)
            var = jnp.where(i == 0, 0, jnp.where(i == SWA_NB - 1, 2, 1))
            q0 = pl.multiple_of(i * SWA_BLOCK, SWA_BLOCK)
            k0 = pl.multiple_of(ws * SWA_BLOCK, SWA_BLOCK)
            lhs = stack(q_ref[pl.ds(q0, SWA_BLOCK), :])
            items.append((lhs, kd_scr[pl.ds(k0, SWA_NK), :], vd_scr[pl.ds(k0, SWA_NK), :], bw_ref[0, var],
                          bm_ref[0, jnp.minimum(i, 1)]))
            starts.append(q0)
        for q0, o in zip(starts, _softmax_pv_group(items, kmp, vmp)):
            o_ref[pl.ds(q0, SWA_BLOCK), :] = unstack(o, SWA_BLOCK).astype(o_ref.dtype)
        return carry

    lax.fori_loop(0, SWA_NB // SWA_UNROLL, blocks, 0)


def _t5_bucket(rel):
    nb = T5_BUCKETS // 2
    max_exact = nb // 2
    ret = jnp.where(rel > 0, nb, 0)
    n = jnp.abs(rel)
    nf = jnp.maximum(n, max_exact).astype(jnp.float32)
    large = max_exact + (jnp.log(nf / max_exact) / math.log(T5_MAX_DIST / max_exact) * (nb - max_exact)).astype(I32)
    large = jnp.minimum(large, nb - 1)
    return ret + jnp.where(n < max_exact, n, large)


def _swa_bias_tables(t5_bias, sink):
    tb = t5_bias.astype(F32)
    sink_hg = sink.astype(F32).reshape(SWA_KV_HEADS, SWA_GROUP)

    def meta_tile(b):
        q = b.shape[1] // SWA_GROUP
        s = jnp.repeat(sink_hg, q, axis=1)[..., None]
        pad = jnp.full(b.shape[:2] + (LANES - N_META - 1,), NEG_INF, F32)
        return jnp.concatenate([b, s, pad], axis=-1)

    def head_bias(rel):
        pick = (_t5_bucket(rel).reshape(-1)[None, :] == jnp.arange(T5_BUCKETS)[:, None]).astype(F32)
        b = jnp.dot(tb.T, pick, precision=lax.Precision.HIGHEST)
        return b.reshape(SWA_KV_HEADS, SWA_GROUP * rel.shape[0], rel.shape[1])

    qi = jnp.arange(SWA_BLOCK)[:, None]
    j = jnp.arange(SWA_NK)[None, :]
    wins = []
    for off in (0, SWA_BLOCK, 2 * SWA_BLOCK):
        rel = j - off - qi
        band = jnp.tile(jnp.abs(rel) <= SWA_WINDOW, (SWA_GROUP, 1))[None]
        wins.append(jnp.where(band, head_bias(rel), NEG_INF))
    bw = jnp.stack(wins, axis=1)
    pos_m = jnp.arange(N_META)[None, :]
    bm = jnp.stack([meta_tile(head_bias(pos_m - (N_META + blk * SWA_BLOCK + qi))) for blk in (0, 1)], axis=1)
    bmm = meta_tile(head_bias(pos_m - jnp.arange(N_META)[:, None]))
    return bw, bm, bmm


def swa_attention(qkv, sink, t5_bias, batch):
    nq = SWA_Q_HEADS * HEAD_DIM
    mrow = batch * SEQ // N_META
    bw, bm, bmm = _swa_bias_tables(t5_bias, sink)
    kcol = nq // LANES
    vcol = kcol + SWA_KV_HEADS * HEAD_DIM // LANES
    pad_tiles = [pltpu.VMEM((LANES, LANES), BF16), pltpu.VMEM((LANES, LANES), BF16)]
    o = pl.pallas_call(
        _swa_kernel,
        out_shape=jax.ShapeDtypeStruct((qkv.shape[0], D_MODEL), BF16),
        grid=(SWA_KV_HEADS, batch),
        in_specs=[
            pl.BlockSpec((SEQ, 2 * LANES), lambda n, b: (b, n)),
            pl.BlockSpec((SEQ, LANES), lambda n, b: (b, kcol + n // 2)),
            pl.BlockSpec((SEQ, LANES), lambda n, b: (b, vcol + n // 2)),
            pl.BlockSpec((N_META, LANES), lambda n, b: (mrow + b, kcol + n // 2)),
            pl.BlockSpec((N_META, LANES), lambda n, b: (mrow + b, vcol + n // 2)),
            pl.BlockSpec((1, 3, SWA_ROWS, SWA_NK), lambda n, b: (n, 0, 0, 0)),
            pl.BlockSpec((1, 2, SWA_ROWS, LANES), lambda n, b: (n, 0, 0, 0)),
        ],
        out_specs=pl.BlockSpec((SEQ, 2 * LANES), lambda n, b: (b, n)),
        scratch_shapes=[pltpu.VMEM((SEQ, LANES), BF16), pltpu.VMEM((SEQ, LANES), BF16)] + pad_tiles,
        compiler_params=_params(("parallel", "parallel")),
        name="swa_attention",
    )(qkv, qkv, qkv, qkv, qkv, bw, bm)
    return pl.pallas_call(
        _swa_meta_kernel,
        out_shape=jax.ShapeDtypeStruct(o.shape, o.dtype),
        grid=(batch,),
        in_specs=[pl.BlockSpec((N_META, qkv.shape[1]), lambda b: (mrow + b, 0)),
                  pl.BlockSpec((SWA_KV_HEADS, SWA_MROWS, LANES), lambda b: (0, 0, 0)),
                  pl.BlockSpec(memory_space=pl.ANY)],
        out_specs=pl.BlockSpec((N_META, D_MODEL), lambda b: (mrow + b, 0)),
        scratch_shapes=pad_tiles,
        input_output_aliases={2: 0},
        compiler_params=_params(("parallel",)),
        name="swa_meta_attention",
    )(qkv, bmm, o)


def _swa_meta_kernel(qkv_ref, bmm_ref, o_hbm, om_ref, kmp_scr, vmp_scr):
    del o_hbm
    lo = lax.broadcasted_iota(I32, (1, LANES), 1) < HEAD_DIM
    kmp_scr[...] = jnp.zeros_like(kmp_scr)
    vmp_scr[...] = jnp.zeros_like(vmp_scr)
    kbase = SWA_Q_HEADS * HEAD_DIM
    vbase = kbase + SWA_KV_HEADS * HEAD_DIM
    for n in range(SWA_KV_HEADS):
        spread = _kv_spread(n % 2)
        pair = slice((n // 2) * LANES, (n // 2 + 1) * LANES)
        kmp_scr[:N_META] = _dup(qkv_ref[:, kbase:vbase][:, pair], spread)
        vmp_scr[:N_META] = _dup(qkv_ref[:, vbase:][:, pair], spread)
        qs = slice(n * 2 * LANES, (n + 1) * 2 * LANES)
        omm = _meta_only(_stack4(qkv_ref[:, qs], lo), kmp_scr[...], vmp_scr[...], bmm_ref[n])
        om_ref[:, qs] = _unstack4(omm, N_META, lo).astype(om_ref.dtype)


def _attn_out_router_kernel(o_ref, w_ref, h_ref, g_ref, wr_ref, out_ref, route_ref):
    lane = lax.broadcasted_iota(I32, (1, LANES), 1).astype(F32)
    for r0 in range(0, out_ref.shape[0], QKV_ROW_CHUNK):
        rows = slice(r0, r0 + QKV_ROW_CHUNK)
        hn = h_ref[rows, :] + jnp.dot(o_ref[rows, :], w_ref[...], preferred_element_type=F32)
        out_ref[rows, :] = hn
        y = _rms_rows(hn, g_ref[...]).astype(BF16)
        logits = jnp.dot(y, wr_ref[...], preferred_element_type=F32)
        logits = jnp.where(lane < N_EXPERTS, logits, -jnp.inf)
        v1 = jnp.max(logits, axis=-1, keepdims=True)
        i1 = jnp.min(jnp.where(logits == v1, lane, float(LANES)), axis=-1, keepdims=True)
        rest = jnp.where(lane == i1, -jnp.inf, logits)
        v2 = jnp.max(rest, axis=-1, keepdims=True)
        i2 = jnp.min(jnp.where(rest == v2, lane, float(LANES)), axis=-1, keepdims=True)
        e2 = jnp.exp(v2 - v1)
        den = 1.0 + e2
        route_ref[rows, :] = jnp.where(
            lane == 0, i1, jnp.where(lane == 1, i2, jnp.where(lane == 2, 1.0 / den, jnp.where(lane == 3, e2 / den, 0.0))))


def attn_out_router(o, w_o, h, g, w_router):
    nt, d = h.shape
    wr = jnp.zeros((d, LANES), BF16).at[:, :N_EXPERTS].set(w_router.astype(BF16))
    row = pl.BlockSpec((ROW_TILE, d), lambda i: (i, 0))
    return pl.pallas_call(
        _attn_out_router_kernel,
        out_shape=(jax.ShapeDtypeStruct((nt, d), F32), jax.ShapeDtypeStruct((nt, LANES), F32)),
        grid=(nt // ROW_TILE,),
        in_specs=[row, pl.BlockSpec((d, d), lambda i: (0, 0)), row, pl.BlockSpec((1, d), lambda i: (0, 0)),
                  pl.BlockSpec((d, LANES), lambda i: (0, 0))],
        out_specs=(row, pl.BlockSpec((ROW_TILE, LANES), lambda i: (i, 0))),
        compiler_params=_params(("parallel",)),
        name="attn_out_router",
    )(o, w_o, h, g.reshape(1, d), wr)


def _silu_mul(g, u):
    return (g / (1.0 + jnp.exp(-g))) * u


def _dense_layer_kernel(o_ref, wo_ref, h_ref, g_ref, wg_ref, wu_ref, wd_ref, out_ref, y_scr):
    f = pl.program_id(1)

    @pl.when(f == 0)
    def _():
        for r0 in range(0, out_ref.shape[0], QKV_ROW_CHUNK):
            rows = slice(r0, r0 + QKV_ROW_CHUNK)
            hn = h_ref[rows, :] + jnp.dot(o_ref[rows, :], wo_ref[...], preferred_element_type=F32)
            out_ref[rows, :] = hn
            y_scr[rows, :] = _rms_rows(hn, g_ref[...]).astype(BF16)

    y = y_scr[...]
    a = _silu_mul(jnp.dot(y, wg_ref[0], preferred_element_type=F32),
                  jnp.dot(y, wu_ref[0], preferred_element_type=F32)).astype(BF16)
    out_ref[...] += jnp.dot(a, wd_ref[0], preferred_element_type=F32)


def dense_layer_tail(o, w_o, h, g, w_gate, w_up, w_down):
    nt, d = h.shape
    row = pl.BlockSpec((ROW_TILE, d), lambda i, f: (i, 0))
    return pl.pallas_call(
        _dense_layer_kernel,
        out_shape=jax.ShapeDtypeStruct((nt, d), F32),
        grid=(nt // ROW_TILE, FFN_DIM // FFN_CHUNK),
        in_specs=[row, pl.BlockSpec((d, d), lambda i, f: (0, 0)), row, pl.BlockSpec((1, d), lambda i, f: (0, 0)),
                  pl.BlockSpec((1, d, FFN_CHUNK), lambda i, f: (f, 0, 0)),
                  pl.BlockSpec((1, d, FFN_CHUNK), lambda i, f: (f, 0, 0)),
                  pl.BlockSpec((1, FFN_CHUNK, d), lambda i, f: (f, 0, 0))],
        out_specs=row,
        scratch_shapes=[pltpu.VMEM((ROW_TILE, d), BF16)],
        compiler_params=_params(("parallel", "arbitrary")),
        name="dense_layer_tail",
    )(o, w_o, h, g.reshape(1, d), w_gate, w_up, w_down)


HALF_D = D_MODEL // 2


def _pack_bf16_pairs(y):
    r = y.astype(BF16).astype(F32)
    hi = lax.bitcast_convert_type(r[:, :HALF_D], U32)
    low = lax.bitcast_convert_type(r[:, HALF_D:], U32)
    return hi | (low >> 16)


def _unpack_bf16_pairs(w):
    hi = lax.bitcast_convert_type(w & jnp.uint32(0xFFFF0000), F32)
    low = lax.bitcast_convert_type(w << 16, F32)
    return jnp.concatenate([hi, low], axis=1).astype(BF16)


SUBLANES = 8


def _dispatch_kernel(dest_hbm, h_ref, g_ref, zeros_hbm, xs_hbm, idx_smem, y_scr, sem_idx, sem_rows, *, n):
    del zeros_hbm
    i = pl.program_id(0)
    slot = i % 2

    ipt = TOP_K * MOE_TILE

    def idx_copy(step, s):
        return pltpu.make_async_copy(dest_hbm.at[step], idx_smem.at[pl.ds(s * ipt, ipt)], sem_idx.at[s])

    def drain_rows(s):
        for _ in range(TOP_K):
            pltpu.make_async_copy(y_scr.at[s], xs_hbm.at[pl.ds(0, MOE_TILE), :], sem_rows.at[s]).wait()

    @pl.when(i == 0)
    def _():
        idx_copy(0, 0).start()

    @pl.when(i + 1 < n)
    def _():
        idx_copy(i + 1, 1 - slot).start()

    @pl.when(i >= 2)
    def _():
        drain_rows(slot)

    y_scr[slot] = _pack_bf16_pairs(_rms_rows(h_ref[...], g_ref[...]))
    idx_copy(i, slot).wait()

    ibase = slot * ipt

    def issue(grp, c):
        t0 = pl.multiple_of(grp * SUBLANES, SUBLANES)
        for u in range(SUBLANES):
            for k in range(TOP_K):
                d = idx_smem[ibase + TOP_K * (t0 + u) + k]
                pltpu.make_async_copy(y_scr.at[slot, pl.ds(t0 + u, 1), :], xs_hbm.at[pl.ds(d, 1), :],
                                      sem_rows.at[slot]).start()
        return c

    lax.fori_loop(0, MOE_TILE // SUBLANES, issue, 0)

    @pl.when(i == n - 1)
    def _():
        drain_rows(slot)
        if n >= 2:
            drain_rows(1 - slot)


def moe_dispatch(h, g, dest, n_rows):
    nt, d = h.shape
    ntiles = nt // MOE_TILE
    zeros = jnp.zeros((n_rows, HALF_D), U32)
    return pl.pallas_call(
        functools.partial(_dispatch_kernel, n=ntiles),
        out_shape=jax.ShapeDtypeStruct((n_rows, HALF_D), U32),
        grid=(ntiles,),
        in_specs=[pl.BlockSpec(memory_space=pl.ANY),
                  pl.BlockSpec((MOE_TILE, d), lambda i: (i, 0)),
                  pl.BlockSpec((1, d), lambda i: (0, 0)),
                  pl.BlockSpec(memory_space=pl.ANY)],
        out_specs=pl.BlockSpec(memory_space=pl.ANY),
        scratch_shapes=[pltpu.SMEM((2 * TOP_K * MOE_TILE,), I32), pltpu.VMEM((2, MOE_TILE, HALF_D), U32),
                        pltpu.SemaphoreType.DMA((2,)), pltpu.SemaphoreType.DMA((2,))],
        input_output_aliases={3: 0},
        compiler_params=_params(("arbitrary",)),
        name="moe_dispatch",
    )(dest.reshape(ntiles, TOP_K * MOE_TILE), h, g.reshape(1, d), zeros)


def _moe_ffn_kernel(blk_exp_ref, nblk_ref, xs_ref, wg_ref, wu_ref, wd_ref, out_ref, x_scr):
    del blk_exp_ref
    i = pl.program_id(0)
    f = pl.program_id(1)
    live = i < nblk_ref[0]

    @pl.when(f == 0)
    def _():
        out_ref[...] = jnp.zeros_like(out_ref)

    @pl.when(jnp.logical_and(live, f == 0))
    def _():
        x_scr[...] = _unpack_bf16_pairs(xs_ref[...])

    @pl.when(live)
    def _():
        x = x_scr[...]
        a = _silu_mul(jnp.dot(x, wg_ref[0, 0], preferred_element_type=F32),
                      jnp.dot(x, wu_ref[0, 0], preferred_element_type=F32)).astype(BF16)
        out_ref[...] += jnp.dot(a, wd_ref[0, 0], preferred_element_type=F32)


def moe_ffn(xs, blk_exp, n_blocks, w_gate, w_up, w_down):
    n_rows = xs.shape[0]
    nb_max = n_rows // MOE_BLOCK
    nf = FFN_DIM // FFN_CHUNK

    def rows(i, f, be, nb):
        return (jnp.minimum(i, nb[0] - 1), 0)

    def fidx(i, f, nb):
        return jnp.where(i < nb[0], f, nf - 1)

    grid_spec = pltpu.PrefetchScalarGridSpec(
        num_scalar_prefetch=2,
        grid=(nb_max, nf),
        in_specs=[
            pl.BlockSpec((MOE_BLOCK, HALF_D), rows),
            pl.BlockSpec((1, 1, D_MODEL, FFN_CHUNK), lambda i, f, be, nb: (be[i], fidx(i, f, nb), 0, 0)),
            pl.BlockSpec((1, 1, D_MODEL, FFN_CHUNK), lambda i, f, be, nb: (be[i], fidx(i, f, nb), 0, 0)),
            pl.BlockSpec((1, 1, FFN_CHUNK, D_MODEL), lambda i, f, be, nb: (be[i], fidx(i, f, nb), 0, 0)),
        ],
        out_specs=pl.BlockSpec((MOE_BLOCK, D_MODEL), lambda i, f, be, nb: (i, 0)),
        scratch_shapes=[pltpu.VMEM((MOE_BLOCK, D_MODEL), BF16)],
    )
    return pl.pallas_call(
        _moe_ffn_kernel,
        out_shape=jax.ShapeDtypeStruct((n_rows, D_MODEL), F32),
        grid_spec=grid_spec,
        compiler_params=_params(("arbitrary", "arbitrary")),
        name="moe_ffn",
    )(blk_exp, n_blocks, xs, w_gate, w_up, w_down)


def _combine_kernel(dest_hbm, h_ref, route_ref, y_hbm, out_ref, idx_smem, rows_scr, sem_idx, sem_rows, *, n):
    i = pl.program_id(0)
    slot = i % 2

    ipt = TOP_K * MOE_TILE

    def idx_copy(step, s):
        return pltpu.make_async_copy(dest_hbm.at[step], idx_smem.at[pl.ds(s * ipt, ipt)], sem_idx.at[s])

    def issue_rows(s):
        ibase = s * ipt

        def issue(grp, c):
            t0 = pl.multiple_of(grp * SUBLANES, SUBLANES)
            for u in range(SUBLANES):
                for k in range(TOP_K):
                    d = idx_smem[ibase + TOP_K * (t0 + u) + k]
                    pltpu.make_async_copy(y_hbm.at[pl.ds(d, 1), :], rows_scr.at[s, k, pl.ds(t0 + u, 1), :],
                                          sem_rows.at[s]).start()
            return c

        lax.fori_loop(0, MOE_TILE // SUBLANES, issue, 0)

    @pl.when(i == 0)
    def _():
        first = idx_copy(0, 0)
        first.start()
        first.wait()
        issue_rows(0)
        if n >= 2:
            idx_copy(1, 1).start()

    @pl.when(i + 1 < n)
    def _():
        idx_copy(i + 1, 1 - slot).wait()
        issue_rows(1 - slot)

    @pl.when(i + 2 < n)
    def _():
        idx_copy(i + 2, slot).start()

    for k in range(TOP_K):
        pltpu.make_async_copy(y_hbm.at[pl.ds(0, MOE_TILE), :], rows_scr.at[slot, k], sem_rows.at[slot]).wait()
    route = route_ref[...]
    g0 = route[:, 2:3]
    g1 = route[:, 3:4]
    out_ref[...] = h_ref[...] + (rows_scr[slot, 0] * g0 + rows_scr[slot, 1] * g1)


def moe_combine(h, route, y, dest, n_out_rows):
    nt, d = h.shape
    row = pl.BlockSpec((MOE_TILE, d), lambda i: (i, 0))
    return pl.pallas_call(
        functools.partial(_combine_kernel, n=n_out_rows // MOE_TILE),
        out_shape=jax.ShapeDtypeStruct((n_out_rows, d), F32),
        grid=(n_out_rows // MOE_TILE,),
        in_specs=[pl.BlockSpec(memory_space=pl.ANY), row,
                  pl.BlockSpec((MOE_TILE, LANES), lambda i: (i, 0)),
                  pl.BlockSpec(memory_space=pl.ANY)],
        out_specs=row,
        scratch_shapes=[pltpu.SMEM((2 * TOP_K * MOE_TILE,), I32), pltpu.VMEM((2, TOP_K, MOE_TILE, d), F32),
                        pltpu.SemaphoreType.DMA((2,)), pltpu.SemaphoreType.DMA((2,))],
        compiler_params=_params(("arbitrary",)),
        name="moe_combine",
    )(dest.reshape(nt // MOE_TILE, TOP_K * MOE_TILE), h, route, y)


def _moe_plan(route, nt):
    experts = route[:, :TOP_K].astype(I32).reshape(-1)
    onehot = (experts[:, None] == jnp.arange(N_EXPERTS, dtype=I32)[None, :]).astype(I32)
    before = jnp.cumsum(onehot, axis=0) - onehot
    rank = jnp.sum(before * onehot, axis=1)
    counts = jnp.sum(onehot, axis=0)
    padded = (counts + MOE_BLOCK - 1) // MOE_BLOCK * MOE_BLOCK
    pend = jnp.cumsum(padded)
    pstart = pend - padded
    dest = jnp.sum(onehot * pstart[None, :], axis=1) + rank
    nb_max = -(-(nt * TOP_K + N_EXPERTS * (MOE_BLOCK - 1)) // MOE_BLOCK)
    blk_start = jnp.arange(nb_max, dtype=I32) * MOE_BLOCK
    blk_exp = jnp.minimum(jnp.sum((blk_start[:, None] >= pend[None, :]).astype(I32), axis=1), N_EXPERTS - 1)
    n_blocks = (pend[-1] // MOE_BLOCK).astype(I32).reshape(1)
    last_exp = jnp.max(jnp.where(counts > 0, jnp.arange(N_EXPERTS, dtype=I32), 0))
    blk_exp = jnp.where(blk_start < pend[-1], blk_exp, last_exp)
    return dest.astype(I32), blk_exp.astype(I32), n_blocks, nb_max * MOE_BLOCK


def _moe_layer(h, route, g, w_gate, w_up, w_down, n_out_rows):
    nt = h.shape[0]
    dest, blk_exp, n_blocks, n_rows = _moe_plan(route, nt)
    xs = moe_dispatch(h, g, dest, n_rows)
    y = moe_ffn(xs, blk_exp, n_blocks, w_gate, w_up, w_down)
    return moe_combine(h, route, y, dest, n_out_rows)


def _ffn_weight_slabs(w_gate, w_up, w_down):
    nf = FFN_DIM // FFN_CHUNK
    lead = w_gate.shape[:-2]

    def cols(w):
        w = w.astype(BF16).reshape(lead + (D_MODEL, nf, FFN_CHUNK))
        return jnp.moveaxis(w, -2, -3)

    return cols(w_gate), cols(w_up), w_down.astype(BF16).reshape(lead + (nf, FFN_CHUNK, D_MODEL))


def _forward(x, meta_tokens, t5_bias, layers):
    batch = x.shape[0]
    nt = batch * (SEQ + N_META)
    h = jnp.concatenate([x.reshape(batch * SEQ, D_MODEL).astype(F32),
                         jnp.tile(meta_tokens.astype(F32), (batch, 1))], axis=0)
    for i, p in enumerate(layers):
        wqkv = p['w_qkv'].astype(BF16)
        if i % 2 == 0:
            qkv = qkv_projection(h, p['norm_mix'], wqkv, p['q_norm'], p['k_norm'], NA_HEADS, NA_HEADS, 512)
            o = na_attention(qkv, p['rpb'], batch)
        else:
            qkv = qkv_projection(h, p['norm_mix'], wqkv, p['q_norm'], p['k_norm'], SWA_Q_HEADS, SWA_KV_HEADS, 512)
            o = swa_attention(qkv, p['sink'], t5_bias, batch)
        w_o = p['w_o'].astype(BF16)
        wg, wu, wd = _ffn_weight_slabs(p['w_gate'], p['w_up'], p['w_down'])
        if i % 2 == 0:
            h = dense_layer_tail(o, w_o, h, p['norm_ffn'], wg, wu, wd)
        else:
            h, route = attn_out_router(o, w_o, h, p['norm_ffn'], p['w_router'])
            last = i == len(layers) - 1
            h = _moe_layer(h, route, p['norm_ffn'], wg, wu, wd, batch * SEQ if last else nt)
    return h[:batch * SEQ].reshape(batch, SEQ, D_MODEL)


def kernel(x, meta_tokens, t5_bias, l0_norm_mix, l0_w_qkv, l0_q_norm, l0_k_norm, l0_rpb, l0_w_o, l0_norm_ffn, l0_w_gate, l0_w_up, l0_w_down, l1_norm_mix, l1_w_qkv, l1_q_norm, l1_k_norm, l1_sink, l1_w_o, l1_norm_ffn, l1_w_router, l1_w_gate, l1_w_up, l1_w_down, l2_norm_mix, l2_w_qkv, l2_q_norm, l2_k_norm, l2_rpb, l2_w_o, l2_norm_ffn, l2_w_gate, l2_w_up, l2_w_down, l3_norm_mix, l3_w_qkv, l3_q_norm, l3_k_norm, l3_sink, l3_w_o, l3_norm_ffn, l3_w_router, l3_w_gate, l3_w_up, l3_w_down):
    layers = [
        dict(norm_mix=l0_norm_mix, w_qkv=l0_w_qkv, q_norm=l0_q_norm, k_norm=l0_k_norm, rpb=l0_rpb, w_o=l0_w_o,
             norm_ffn=l0_norm_ffn, w_gate=l0_w_gate, w_up=l0_w_up, w_down=l0_w_down),
        dict(norm_mix=l1_norm_mix, w_qkv=l1_w_qkv, q_norm=l1_q_norm, k_norm=l1_k_norm, sink=l1_sink, w_o=l1_w_o,
             norm_ffn=l1_norm_ffn, w_router=l1_w_router, w_gate=l1_w_gate, w_up=l1_w_up, w_down=l1_w_down),
        dict(norm_mix=l2_norm_mix, w_qkv=l2_w_qkv, q_norm=l2_q_norm, k_norm=l2_k_norm, rpb=l2_rpb, w_o=l2_w_o,
             norm_ffn=l2_norm_ffn, w_gate=l2_w_gate, w_up=l2_w_up, w_down=l2_w_down),
        dict(norm_mix=l3_norm_mix, w_qkv=l3_w_qkv, q_norm=l3_q_norm, k_norm=l3_k_norm, sink=l3_sink, w_o=l3_w_o,
             norm_ffn=l3_norm_ffn, w_router=l3_w_router, w_gate=l3_w_gate, w_up=l3_w_up, w_down=l3_w_down),
    ]
    return _forward(x, meta_tokens, t5_bias, layers)
```
